```python
import math
import jax, jax.numpy as jnp
from jax import lax
import numpy as np

D_MODEL = 2048
BATCH = 8
SEQ = 8192
DEPTH = 4

HEAD_DIM = 128
MLA_HEADS = 8
MLA_Q_RANK = 512
MLA_KV_RANK = 512
MLA_NOPE_DIM = 128
MLA_ROPE_DIM = 64
MLA_V_DIM = 128
MLA_QK_DIM = MLA_NOPE_DIM + MLA_ROPE_DIM
DIL_HEADS = 8
DIL_PATTERNS = ((128, 1), (512, 4), (2048, 16))
ROPE_THETA = 500000.0
PARTIAL_ROPE_DIM = HEAD_DIM // 4
D_FF = 5632
Q_BLOCK = 128
RMS_EPS = 1e-6
NEG = -1e30
MLA_WIDTH = MLA_HEADS * MLA_V_DIM
DIL_WIDTH = DIL_HEADS * HEAD_DIM
D_MIX = MLA_WIDTH + DIL_WIDTH
IN_SIZES = (MLA_Q_RANK, MLA_KV_RANK, MLA_ROPE_DIM, DIL_WIDTH, DIL_WIDTH, DIL_WIDTH)
IN_COLS = sum(IN_SIZES)
IN_SPLITS = tuple(int(v) for v in np.cumsum(IN_SIZES)[:-1])

kernel_name = "hymba_mla_dilated_macaron_sandwich"


def rms_norm(x, g):
    xf = x.astype(jnp.float32)
    y = xf * lax.rsqrt(jnp.mean(xf * xf, axis=-1, keepdims=True) + RMS_EPS)
    return (y * g.astype(jnp.float32)).astype(x.dtype)


def swiglu(h, w_gate, w_up, w_down):
    return (jax.nn.silu(h @ w_gate) * (h @ w_up)) @ w_down


def rope_tables(positions, dim):
    inv = ROPE_THETA ** (-jnp.arange(0, dim, 2, dtype=jnp.float32) / dim)
    ang = positions.astype(jnp.float32)[..., None] * inv
    return jnp.cos(ang), jnp.sin(ang)


def apply_rope(x, cos, sin):
    xf = x.astype(jnp.float32)
    x1, x2 = jnp.split(xf, 2, axis=-1)
    out = jnp.concatenate([x1 * cos - x2 * sin, x2 * cos + x1 * sin], axis=-1)
    return out.astype(x.dtype)


def mla_attention(q_nope, q_rope, k_nope, k_rope, v):
    B, S, H, _ = q_nope.shape
    nb = S // Q_BLOCK
    scale = 1.0 / math.sqrt(MLA_QK_DIM)
    key_idx = jnp.arange(S)

    def to_blocks(t):
        return jnp.moveaxis(t.reshape(B, nb, Q_BLOCK, *t.shape[2:]), 1, 0)

    def one_block(args):
        qn_b, qr_b, i = args
        s = (jnp.einsum('bqhd,bkhd->bhqk', qn_b, k_nope).astype(jnp.float32)
             + jnp.einsum('bqhr,bkr->bhqk', qr_b, k_rope).astype(jnp.float32)) * scale
        q_idx = i * Q_BLOCK + jnp.arange(Q_BLOCK)
        mask = key_idx[None, :] <= q_idx[:, None]
        s = jnp.where(mask, s, NEG)
        p = jax.nn.softmax(s, axis=-1).astype(v.dtype)
        return jnp.einsum('bhqk,bkhd->bqhd', p, v)

    o = lax.map(one_block, (to_blocks(q_nope), to_blocks(q_rope), jnp.arange(nb)))
    return jnp.moveaxis(o, 0, 1).reshape(B, S, H, -1)


def dilated_window_attention(q, k, v, window, dilation):
    B, S, H, Dh = q.shape
    span = window // dilation
    blk = span
    seg = blk * dilation
    s_pad = -(-S // seg) * seg
    L = s_pad // dilation
    nb = L // blk
    scale = 1.0 / math.sqrt(Dh)

    def to_sub(t):
        t = jnp.pad(t, ((0, 0), (0, s_pad - S), (0, 0), (0, 0)))
        t = t.reshape(B, L, dilation, H, Dh).transpose(0, 2, 1, 3, 4)
        return t.reshape(B, dilation, nb, blk, H, Dh)

    def with_prev(t):
        prev = jnp.pad(t[:, :, :-1], ((0, 0), (0, 0), (1, 0), (0, 0), (0, 0), (0, 0)))
        return jnp.concatenate([prev, t], axis=3)

    qb = to_sub(q)
    kc = with_prev(to_sub(k))
    vc = with_prev(to_sub(v))
    s = jnp.einsum('brnqhd,brnkhd->brnhqk', qb, kc).astype(jnp.float32) * scale
    qi = jnp.arange(blk)[:, None]
    kj = jnp.arange(2 * blk)[None, :]
    dist = blk + qi - kj
    band = (dist >= 0) & (dist <= span)
    valid_prev = (jnp.arange(nb)[:, None, None] > 0) | (kj >= blk)[None]
    mask = band[None] & valid_prev
    s = jnp.where(mask[None, None, :, None], s, NEG)
    lse = jax.nn.logsumexp(s, axis=-1)
    p = jnp.exp(s - lse[..., None]).astype(v.dtype)
    o = jnp.einsum('brnhqk,brnkhd->brnqhd', p, vc)

    def from_sub(t):
        rest = t.shape[4:]
        t = t.reshape(B, dilation, L, *rest)
        t = jnp.moveaxis(t, 1, 2).reshape(B, s_pad, *rest)
        return t[:, :S]

    return from_sub(o), from_sub(jnp.moveaxis(lse, -1, -2))


def dilated_mixture(q, k, v):
    outs, lses = [], []
    for window, dilation in DIL_PATTERNS:
        o_p, lse_p = dilated_window_attention(q, k, v, window, dilation)
        outs.append(o_p)
        lses.append(lse_p)
    w = jax.nn.softmax(jnp.stack(lses, axis=0), axis=0)
    return jnp.einsum('pbsh,pbshd->bshd', w.astype(q.dtype), jnp.stack(outs, axis=0))


def _fwd_setup_inputs(seed: int = 0) -> dict:
    key = jax.random.key(seed)
    ks = jax.random.split(key, 24)

    def w(k, shape, fan_in):
        return jax.random.normal(k, shape, jnp.float32) * (fan_in ** -0.5)

    def gain(k, n):
        return 1.0 + 0.1 * jax.random.normal(k, (DEPTH, n), jnp.float32)

    x = jax.random.normal(ks[0], (BATCH, SEQ, D_MODEL), jnp.float32)
    offset = jax.random.randint(ks[1], (BATCH, 1), 0, 1024, dtype=jnp.int32)
    positions = (offset + jnp.arange(SEQ, dtype=jnp.int32)[None, :]).astype(jnp.int32)
    return {
        "x": x,
        "positions": positions,
        "ffn1_pre_g": gain(ks[2], D_MODEL),
        "ffn1_post_g": gain(ks[3], D_MODEL),
        "ffn1_w_gate": w(ks[4], (DEPTH, D_MODEL, D_FF), D_MODEL),
        "ffn1_w_up": w(ks[5], (DEPTH, D_MODEL, D_FF), D_MODEL),
        "ffn1_w_down": w(ks[6], (DEPTH, D_FF, D_MODEL), D_FF),
        "mix_pre_g": gain(ks[7], D_MODEL),
        "mix_post_g": gain(ks[8], D_MODEL),
        "w_in": w(ks[9], (DEPTH, D_MODEL, IN_COLS), D_MODEL),
        "mla_q_norm_g": gain(ks[10], MLA_Q_RANK),
        "mla_w_uq": w(ks[11], (DEPTH, MLA_Q_RANK, MLA_HEADS * MLA_QK_DIM), MLA_Q_RANK),
        "mla_kv_norm_g": gain(ks[12], MLA_KV_RANK),
        "mla_w_ukv": w(ks[13], (DEPTH, MLA_KV_RANK, MLA_HEADS * (MLA_NOPE_DIM + MLA_V_DIM)), MLA_KV_RANK),
        "w_o": w(ks[14], (DEPTH, D_MIX, D_MODEL), D_MIX),
        "ffn2_pre_g": gain(ks[15], D_MODEL),
        "ffn2_post_g": gain(ks[16], D_MODEL),
        "ffn2_w_gate": w(ks[17], (DEPTH, D_MODEL, D_FF), D_MODEL),
        "ffn2_w_up": w(ks[18], (DEPTH, D_MODEL, D_FF), D_MODEL),
        "ffn2_w_down": w(ks[19], (DEPTH, D_FF, D_MODEL), D_FF),
    }


def _fwd_reference(x, positions, ffn1_pre_g, ffn1_post_g, ffn1_w_gate, ffn1_w_up, ffn1_w_down,
              mix_pre_g, mix_post_g, w_in, mla_q_norm_g, mla_w_uq, mla_kv_norm_g, mla_w_ukv,
              w_o, ffn2_pre_g, ffn2_post_g, ffn2_w_gate, ffn2_w_up, ffn2_w_down):
    B, S, _ = x.shape
    cos_a, sin_a = rope_tables(positions, MLA_ROPE_DIM)
    cos_p, sin_p = rope_tables(positions, PARTIAL_ROPE_DIM)

    def partial_rope(t):
        return jnp.concatenate([apply_rope(t[..., :PARTIAL_ROPE_DIM], cos_p[:, :, None], sin_p[:, :, None]),
                                t[..., PARTIAL_ROPE_DIM:]], axis=-1)

    for l in range(DEPTH):
        h = rms_norm(x, ffn1_pre_g[l])
        x = x + 0.5 * rms_norm(swiglu(h, ffn1_w_gate[l], ffn1_w_up[l], ffn1_w_down[l]), ffn1_post_g[l])

        h = rms_norm(x, mix_pre_g[l])
        proj = h @ w_in[l]
        c_q, c_kv, k_rope, q_d, k_d, v_d = jnp.split(proj, IN_SPLITS, axis=-1)

        q_a = (rms_norm(c_q, mla_q_norm_g[l]) @ mla_w_uq[l]).reshape(B, S, MLA_HEADS, MLA_QK_DIM)
        q_nope, q_rope = q_a[..., :MLA_NOPE_DIM], q_a[..., MLA_NOPE_DIM:]
        q_rope = apply_rope(q_rope, cos_a[:, :, None], sin_a[:, :, None])
        k_rope = apply_rope(k_rope, cos_a, sin_a)
        kv = (rms_norm(c_kv, mla_kv_norm_g[l]) @ mla_w_ukv[l]).reshape(B, S, MLA_HEADS, MLA_NOPE_DIM + MLA_V_DIM)
        k_nope, v_a = kv[..., :MLA_NOPE_DIM], kv[..., MLA_NOPE_DIM:]
        o_a = mla_attention(q_nope, q_rope, k_nope, k_rope, v_a)

        q_b = partial_rope(q_d.reshape(B, S, DIL_HEADS, HEAD_DIM))
        k_b = partial_rope(k_d.reshape(B, S, DIL_HEADS, HEAD_DIM))
        v_b = v_d.reshape(B, S, DIL_HEADS, HEAD_DIM)
        o_b = dilated_mixture(q_b, k_b, v_b)

        o = jnp.concatenate([o_a.reshape(B, S, MLA_WIDTH), o_b.reshape(B, S, DIL_WIDTH)], axis=-1) @ w_o[l]
        x = x + rms_norm(o, mix_post_g[l])

        h = rms_norm(x, ffn2_pre_g[l])
        x = x + 0.5 * rms_norm(swiglu(h, ffn2_w_gate[l], ffn2_w_up[l], ffn2_w_down[l]), ffn2_post_g[l])
    return x


import jax as _jax
import jax.numpy as _jnp

TWIN_FORMAT = 'train_step'
FWD_PARAMS = ['x', 'positions', 'ffn1_pre_g', 'ffn1_post_g', 'ffn1_w_gate', 'ffn1_w_up', 'ffn1_w_down', 'mix_pre_g', 'mix_post_g', 'w_in', 'mla_q_norm_g', 'mla_w_uq', 'mla_kv_norm_g', 'mla_w_ukv', 'w_o', 'ffn2_pre_g', 'ffn2_post_g', 'ffn2_w_gate', 'ffn2_w_up', 'ffn2_w_down']
TWIN_WEIGHTS = ['ffn1_pre_g', 'ffn1_post_g', 'ffn1_w_gate', 'ffn1_w_up', 'ffn1_w_down', 'mix_pre_g', 'mix_post_g', 'w_in', 'mla_q_norm_g', 'mla_w_uq', 'mla_kv_norm_g', 'mla_w_ukv', 'w_o', 'ffn2_pre_g', 'ffn2_post_g', 'ffn2_w_gate', 'ffn2_w_up', 'ffn2_w_down']
TWIN_DIFF_INPUT = 'x'
TWIN_INPUTS = ['x', 'positions', 'ffn1_pre_g', 'ffn1_post_g', 'ffn1_w_gate', 'ffn1_w_up', 'ffn1_w_down', 'mix_pre_g', 'mix_post_g', 'w_in', 'mla_q_norm_g', 'mla_w_uq', 'mla_kv_norm_g', 'mla_w_ukv', 'w_o', 'ffn2_pre_g', 'ffn2_post_g', 'ffn2_w_gate', 'ffn2_w_up', 'ffn2_w_down', 'loss_target', 'm_ffn1_pre_g', 'm_ffn1_post_g', 'm_ffn1_w_gate', 'm_ffn1_w_up', 'm_ffn1_w_down', 'm_mix_pre_g', 'm_mix_post_g', 'm_w_in', 'm_mla_q_norm_g', 'm_mla_w_uq', 'm_mla_kv_norm_g', 'm_mla_w_ukv', 'm_w_o', 'm_ffn2_pre_g', 'm_ffn2_post_g', 'm_ffn2_w_gate', 'm_ffn2_w_up', 'm_ffn2_w_down', 'v_ffn1_pre_g', 'v_ffn1_post_g', 'v_ffn1_w_gate', 'v_ffn1_w_up', 'v_ffn1_w_down', 'v_mix_pre_g', 'v_mix_post_g', 'v_w_in', 'v_mla_q_norm_g', 'v_mla_w_uq', 'v_mla_kv_norm_g', 'v_mla_w_ukv', 'v_w_o', 'v_ffn2_pre_g', 'v_ffn2_post_g', 'v_ffn2_w_gate', 'v_ffn2_w_up', 'v_ffn2_w_down']
TWIN_OUTPUTS = ['loss', 'grad_x', 'grad_ffn1_pre_g', 'grad_ffn1_post_g', 'grad_ffn1_w_gate', 'grad_ffn1_w_up', 'grad_ffn1_w_down', 'grad_mix_pre_g', 'grad_mix_post_g', 'grad_w_in', 'grad_mla_q_norm_g', 'grad_mla_w_uq', 'grad_mla_kv_norm_g', 'grad_mla_w_ukv', 'grad_w_o', 'grad_ffn2_pre_g', 'grad_ffn2_post_g', 'grad_ffn2_w_gate', 'grad_ffn2_w_up', 'grad_ffn2_w_down', 'delta_ffn1_pre_g', 'delta_ffn1_post_g', 'delta_ffn1_w_gate', 'delta_ffn1_w_up', 'delta_ffn1_w_down', 'delta_mix_pre_g', 'delta_mix_post_g', 'delta_w_in', 'delta_mla_q_norm_g', 'delta_mla_w_uq', 'delta_mla_kv_norm_g', 'delta_mla_w_ukv', 'delta_w_o', 'delta_ffn2_pre_g', 'delta_ffn2_post_g', 'delta_ffn2_w_gate', 'delta_ffn2_w_up', 'delta_ffn2_w_down', 'new_m_ffn1_pre_g', 'new_m_ffn1_post_g', 'new_m_ffn1_w_gate', 'new_m_ffn1_w_up', 'new_m_ffn1_w_down', 'new_m_mix_pre_g', 'new_m_mix_post_g', 'new_m_w_in', 'new_m_mla_q_norm_g', 'new_m_mla_w_uq', 'new_m_mla_kv_norm_g', 'new_m_mla_w_ukv', 'new_m_w_o', 'new_m_ffn2_pre_g', 'new_m_ffn2_post_g', 'new_m_ffn2_w_gate', 'new_m_ffn2_w_up', 'new_m_ffn2_w_down', 'new_v_ffn1_pre_g', 'new_v_ffn1_post_g', 'new_v_ffn1_w_gate', 'new_v_ffn1_w_up', 'new_v_ffn1_w_down', 'new_v_mix_pre_g', 'new_v_mix_post_g', 'new_v_w_in', 'new_v_mla_q_norm_g', 'new_v_mla_w_uq', 'new_v_mla_kv_norm_g', 'new_v_mla_w_ukv', 'new_v_w_o', 'new_v_ffn2_pre_g', 'new_v_ffn2_post_g', 'new_v_ffn2_w_gate', 'new_v_ffn2_w_up', 'new_v_ffn2_w_down']
TWIN_LEAF_KINDS = {'loss': 'loss', 'grad_x': 'grad_x', 'grad_ffn1_pre_g': 'grad_w', 'grad_ffn1_post_g': 'grad_w', 'grad_ffn1_w_gate': 'grad_w', 'grad_ffn1_w_up': 'grad_w', 'grad_ffn1_w_down': 'grad_w', 'grad_mix_pre_g': 'grad_w', 'grad_mix_post_g': 'grad_w', 'grad_w_in': 'grad_w', 'grad_mla_q_norm_g': 'grad_w', 'grad_mla_w_uq': 'grad_w', 'grad_mla_kv_norm_g': 'grad_w', 'grad_mla_w_ukv': 'grad_w', 'grad_w_o': 'grad_w', 'grad_ffn2_pre_g': 'grad_w', 'grad_ffn2_post_g': 'grad_w', 'grad_ffn2_w_gate': 'grad_w', 'grad_ffn2_w_up': 'grad_w', 'grad_ffn2_w_down': 'grad_w', 'delta_ffn1_pre_g': 'delta_w', 'delta_ffn1_post_g': 'delta_w', 'delta_ffn1_w_gate': 'delta_w', 'delta_ffn1_w_up': 'delta_w', 'delta_ffn1_w_down': 'delta_w', 'delta_mix_pre_g': 'delta_w', 'delta_mix_post_g': 'delta_w', 'delta_w_in': 'delta_w', 'delta_mla_q_norm_g': 'delta_w', 'delta_mla_w_uq': 'delta_w', 'delta_mla_kv_norm_g': 'delta_w', 'delta_mla_w_ukv': 'delta_w', 'delta_w_o': 'delta_w', 'delta_ffn2_pre_g': 'delta_w', 'delta_ffn2_post_g': 'delta_w', 'delta_ffn2_w_gate': 'delta_w', 'delta_ffn2_w_up': 'delta_w', 'delta_ffn2_w_down': 'delta_w', 'new_m_ffn1_pre_g': 'new_m', 'new_m_ffn1_post_g': 'new_m', 'new_m_ffn1_w_gate': 'new_m', 'new_m_ffn1_w_up': 'new_m', 'new_m_ffn1_w_down': 'new_m', 'new_m_mix_pre_g': 'new_m', 'new_m_mix_post_g': 'new_m', 'new_m_w_in': 'new_m', 'new_m_mla_q_norm_g': 'new_m', 'new_m_mla_w_uq': 'new_m', 'new_m_mla_kv_norm_g': 'new_m', 'new_m_mla_w_ukv': 'new_m', 'new_m_w_o': 'new_m', 'new_m_ffn2_pre_g': 'new_m', 'new_m_ffn2_post_g': 'new_m', 'new_m_ffn2_w_gate': 'new_m', 'new_m_ffn2_w_up': 'new_m', 'new_m_ffn2_w_down': 'new_m', 'new_v_ffn1_pre_g': 'new_v', 'new_v_ffn1_post_g': 'new_v', 'new_v_ffn1_w_gate': 'new_v', 'new_v_ffn1_w_up': 'new_v', 'new_v_ffn1_w_down': 'new_v', 'new_v_mix_pre_g': 'new_v', 'new_v_mix_post_g': 'new_v', 'new_v_w_in': 'new_v', 'new_v_mla_q_norm_g': 'new_v', 'new_v_mla_w_uq': 'new_v', 'new_v_mla_kv_norm_g': 'new_v', 'new_v_mla_w_ukv': 'new_v', 'new_v_w_o': 'new_v', 'new_v_ffn2_pre_g': 'new_v', 'new_v_ffn2_post_g': 'new_v', 'new_v_ffn2_w_gate': 'new_v', 'new_v_ffn2_w_up': 'new_v', 'new_v_ffn2_w_down': 'new_v'}


def _forward(args):
    return _fwd_reference(*[args[k] for k in FWD_PARAMS])


def _output_shape():
    def fwd():
        inp = _fwd_setup_inputs(0)
        return _fwd_reference(*[inp[k] for k in FWD_PARAMS])
    out = _jax.eval_shape(fwd)
    return out.shape, out.dtype

N_MICROBATCH = 1
ADAM_LR = 0.001
ADAM_B1 = 0.9
ADAM_B2 = 0.999
ADAM_EPS = 1e-08
ADAM_WD = 0.01
ADAM_STEP = 10
PER_EXAMPLE_BATCH_AXIS = {'x': 0, 'positions': 0, 'loss_target': 0}
SHARED_INPUTS = []
_WEIGHT_DTYPES = {'ffn1_pre_g': _jnp.float32, 'ffn1_post_g': _jnp.float32, 'ffn1_w_gate': _jnp.float32, 'ffn1_w_up': _jnp.float32, 'ffn1_w_down': _jnp.float32, 'mix_pre_g': _jnp.float32, 'mix_post_g': _jnp.float32, 'w_in': _jnp.float32, 'mla_q_norm_g': _jnp.float32, 'mla_w_uq': _jnp.float32, 'mla_kv_norm_g': _jnp.float32, 'mla_w_ukv': _jnp.float32, 'w_o': _jnp.float32, 'ffn2_pre_g': _jnp.float32, 'ffn2_post_g': _jnp.float32, 'ffn2_w_gate': _jnp.float32, 'ffn2_w_up': _jnp.float32, 'ffn2_w_down': _jnp.float32}
MOMENT_SCALE = {'ffn1_pre_g': 7.819077e+00, 'ffn1_post_g': 9.766508e+00, 'ffn1_w_gate': 3.170300e+00, 'ffn1_w_up': 3.253204e+00, 'ffn1_w_down': 5.502822e+00, 'mix_pre_g': 1.696062e+01, 'mix_post_g': 3.763198e+01, 'w_in': 1.209369e+01, 'mla_q_norm_g': 1.629433e+00, 'mla_w_uq': 8.249601e-01, 'mla_kv_norm_g': 2.610733e+01, 'mla_w_ukv': 1.315232e+01, 'w_o': 1.775252e+01, 'ffn2_pre_g': 3.409027e+00, 'ffn2_post_g': 8.226404e+00, 'ffn2_w_gate': 1.216034e+00, 'ffn2_w_up': 1.659006e+00, 'ffn2_w_down': 2.756167e+00}


def _to_microbatches(a, axis):
    t = _jnp.moveaxis(a, axis, 0)
    t = t.reshape((N_MICROBATCH, t.shape[0] // N_MICROBATCH) + t.shape[1:])
    return _jnp.moveaxis(t, 1, axis + 1)


def setup_inputs(seed: int = 0) -> dict:
    inp = _fwd_setup_inputs(seed)
    key = _jax.random.fold_in(_jax.random.key(seed), 7919)
    shape, _ = _output_shape()
    out = dict(inp)
    out["loss_target"] = _jax.random.normal(_jax.random.fold_in(key, 0), shape, _jnp.float32)
    for i, name in enumerate(TWIN_WEIGHTS):
        w = inp[name].astype(_jnp.float32)
        if MOMENT_SCALE is None:
            s = _jnp.sqrt(_jnp.mean(_jnp.square(w)) + 1e-30)
        else:
            s = MOMENT_SCALE[name]
        km, kv = _jax.random.split(_jax.random.fold_in(key, i + 1))
        out[name] = w
        out["m_" + name] = s * _jax.random.normal(km, w.shape, _jnp.float32)
        out["v_" + name] = (s * s) * _jax.random.uniform(kv, w.shape, _jnp.float32, 0.5, 1.5)
    if N_MICROBATCH > 1:
        for name, axis in PER_EXAMPLE_BATCH_AXIS.items():
            out[name] = _to_microbatches(out[name], axis)
    return {'x': out['x'], 'positions': out['positions'], 'ffn1_pre_g': out['ffn1_pre_g'], 'ffn1_post_g': out['ffn1_post_g'], 'ffn1_w_gate': out['ffn1_w_gate'], 'ffn1_w_up': out['ffn1_w_up'], 'ffn1_w_down': out['ffn1_w_down'], 'mix_pre_g': out['mix_pre_g'], 'mix_post_g': out['mix_post_g'], 'w_in': out['w_in'], 'mla_q_norm_g': out['mla_q_norm_g'], 'mla_w_uq': out['mla_w_uq'], 'mla_kv_norm_g': out['mla_kv_norm_g'], 'mla_w_ukv': out['mla_w_ukv'], 'w_o': out['w_o'], 'ffn2_pre_g': out['ffn2_pre_g'], 'ffn2_post_g': out['ffn2_post_g'], 'ffn2_w_gate': out['ffn2_w_gate'], 'ffn2_w_up': out['ffn2_w_up'], 'ffn2_w_down': out['ffn2_w_down'], 'loss_target': out['loss_target'], 'm_ffn1_pre_g': out['m_ffn1_pre_g'], 'm_ffn1_post_g': out['m_ffn1_post_g'], 'm_ffn1_w_gate': out['m_ffn1_w_gate'], 'm_ffn1_w_up': out['m_ffn1_w_up'], 'm_ffn1_w_down': out['m_ffn1_w_down'], 'm_mix_pre_g': out['m_mix_pre_g'], 'm_mix_post_g': out['m_mix_post_g'], 'm_w_in': out['m_w_in'], 'm_mla_q_norm_g': out['m_mla_q_norm_g'], 'm_mla_w_uq': out['m_mla_w_uq'], 'm_mla_kv_norm_g': out['m_mla_kv_norm_g'], 'm_mla_w_ukv': out['m_mla_w_ukv'], 'm_w_o': out['m_w_o'], 'm_ffn2_pre_g': out['m_ffn2_pre_g'], 'm_ffn2_post_g': out['m_ffn2_post_g'], 'm_ffn2_w_gate': out['m_ffn2_w_gate'], 'm_ffn2_w_up': out['m_ffn2_w_up'], 'm_ffn2_w_down': out['m_ffn2_w_down'], 'v_ffn1_pre_g': out['v_ffn1_pre_g'], 'v_ffn1_post_g': out['v_ffn1_post_g'], 'v_ffn1_w_gate': out['v_ffn1_w_gate'], 'v_ffn1_w_up': out['v_ffn1_w_up'], 'v_ffn1_w_down': out['v_ffn1_w_down'], 'v_mix_pre_g': out['v_mix_pre_g'], 'v_mix_post_g': out['v_mix_post_g'], 'v_w_in': out['v_w_in'], 'v_mla_q_norm_g': out['v_mla_q_norm_g'], 'v_mla_w_uq': out['v_mla_w_uq'], 'v_mla_kv_norm_g': out['v_mla_kv_norm_g'], 'v_mla_w_ukv': out['v_mla_w_ukv'], 'v_w_o': out['v_w_o'], 'v_ffn2_pre_g': out['v_ffn2_pre_g'], 'v_ffn2_post_g': out['v_ffn2_post_g'], 'v_ffn2_w_gate': out['v_ffn2_w_gate'], 'v_ffn2_w_up': out['v_ffn2_w_up'], 'v_ffn2_w_down': out['v_ffn2_w_down']}


def _loss(weights, diff, rest, loss_target):
    with _jax.named_scope("forward"):
        args = {**rest, TWIN_DIFF_INPUT: diff, **{k: w.astype(_WEIGHT_DTYPES[k]) for k, w in weights.items()}}
        y = _forward(args)
    with _jax.named_scope("loss_head"):
        err = _jnp.square(y.astype(_jnp.float32) - loss_target)
        return 0.5 * _jnp.sum(_jnp.mean(err, axis=-1)) if err.ndim else 0.5 * err


def _adamw(w, g, m, v):
    m = ADAM_B1 * m + (1.0 - ADAM_B1) * g
    v = ADAM_B2 * v + (1.0 - ADAM_B2) * _jnp.square(g)
    m_hat = m / (1.0 - ADAM_B1 ** ADAM_STEP)
    v_hat = v / (1.0 - ADAM_B2 ** ADAM_STEP)
    delta = -ADAM_LR * (m_hat / (_jnp.sqrt(v_hat) + ADAM_EPS) + ADAM_WD * w)
    return delta, m, v


def reference(x, positions, ffn1_pre_g, ffn1_post_g, ffn1_w_gate, ffn1_w_up, ffn1_w_down, mix_pre_g, mix_post_g, w_in, mla_q_norm_g, mla_w_uq, mla_kv_norm_g, mla_w_ukv, w_o, ffn2_pre_g, ffn2_post_g, ffn2_w_gate, ffn2_w_up, ffn2_w_down, loss_target, m_ffn1_pre_g, m_ffn1_post_g, m_ffn1_w_gate, m_ffn1_w_up, m_ffn1_w_down, m_mix_pre_g, m_mix_post_g, m_w_in, m_mla_q_norm_g, m_mla_w_uq, m_mla_kv_norm_g, m_mla_w_ukv, m_w_o, m_ffn2_pre_g, m_ffn2_post_g, m_ffn2_w_gate, m_ffn2_w_up, m_ffn2_w_down, v_ffn1_pre_g, v_ffn1_post_g, v_ffn1_w_gate, v_ffn1_w_up, v_ffn1_w_down, v_mix_pre_g, v_mix_post_g, v_w_in, v_mla_q_norm_g, v_mla_w_uq, v_mla_kv_norm_g, v_mla_w_ukv, v_w_o, v_ffn2_pre_g, v_ffn2_post_g, v_ffn2_w_gate, v_ffn2_w_up, v_ffn2_w_down):
    given = dict(x=x, positions=positions, ffn1_pre_g=ffn1_pre_g, ffn1_post_g=ffn1_post_g, ffn1_w_gate=ffn1_w_gate, ffn1_w_up=ffn1_w_up, ffn1_w_down=ffn1_w_down, mix_pre_g=mix_pre_g, mix_post_g=mix_post_g, w_in=w_in, mla_q_norm_g=mla_q_norm_g, mla_w_uq=mla_w_uq, mla_kv_norm_g=mla_kv_norm_g, mla_w_ukv=mla_w_ukv, w_o=w_o, ffn2_pre_g=ffn2_pre_g, ffn2_post_g=ffn2_post_g, ffn2_w_gate=ffn2_w_gate, ffn2_w_up=ffn2_w_up, ffn2_w_down=ffn2_w_down, loss_target=loss_target, m_ffn1_pre_g=m_ffn1_pre_g, m_ffn1_post_g=m_ffn1_post_g, m_ffn1_w_gate=m_ffn1_w_gate, m_ffn1_w_up=m_ffn1_w_up, m_ffn1_w_down=m_ffn1_w_down, m_mix_pre_g=m_mix_pre_g, m_mix_post_g=m_mix_post_g, m_w_in=m_w_in, m_mla_q_norm_g=m_mla_q_norm_g, m_mla_w_uq=m_mla_w_uq, m_mla_kv_norm_g=m_mla_kv_norm_g, m_mla_w_ukv=m_mla_w_ukv, m_w_o=m_w_o, m_ffn2_pre_g=m_ffn2_pre_g, m_ffn2_post_g=m_ffn2_post_g, m_ffn2_w_gate=m_ffn2_w_gate, m_ffn2_w_up=m_ffn2_w_up, m_ffn2_w_down=m_ffn2_w_down, v_ffn1_pre_g=v_ffn1_pre_g, v_ffn1_post_g=v_ffn1_post_g, v_ffn1_w_gate=v_ffn1_w_gate, v_ffn1_w_up=v_ffn1_w_up, v_ffn1_w_down=v_ffn1_w_down, v_mix_pre_g=v_mix_pre_g, v_mix_post_g=v_mix_post_g, v_w_in=v_w_in, v_mla_q_norm_g=v_mla_q_norm_g, v_mla_w_uq=v_mla_w_uq, v_mla_kv_norm_g=v_mla_kv_norm_g, v_mla_w_ukv=v_mla_w_ukv, v_w_o=v_w_o, v_ffn2_pre_g=v_ffn2_pre_g, v_ffn2_post_g=v_ffn2_post_g, v_ffn2_w_gate=v_ffn2_w_gate, v_ffn2_w_up=v_ffn2_w_up, v_ffn2_w_down=v_ffn2_w_down)
    weights = {n: given[n] for n in TWIN_WEIGHTS}
    shared = {n: given[n] for n in SHARED_INPUTS}
    per_example = {n: given[n] for n in ['x', 'positions']}
    grad_fn = _jax.value_and_grad(_loss, argnums=(0, 1))

    def one_microbatch(ex, loss_target):
        ex = dict(ex)
        diff = ex.pop(TWIN_DIFF_INPUT)
        return grad_fn(weights, diff, {**shared, **ex}, loss_target)

    if N_MICROBATCH == 1:
        loss, (grad_w, grad_x) = one_microbatch(per_example, given["loss_target"])
    else:
        def body(carry, xs):
            loss_sum, grad_sum = carry
            l_k, (gw_k, gx_k) = one_microbatch(xs[0], xs[1])
            with _jax.named_scope("update"):
                return (loss_sum + l_k, _jax.tree.map(_jnp.add, grad_sum, gw_k)), gx_k

        init = (_jnp.zeros((), _jnp.float32), _jax.tree.map(_jnp.zeros_like, weights))
        (loss, grad_w), grad_x = _jax.lax.scan(body, init, (per_example, given["loss_target"]))
    with _jax.named_scope("update"):
        delta_w, new_m, new_v = {}, {}, {}
        for n in TWIN_WEIGHTS:
            delta_w[n], new_m[n], new_v[n] = _adamw(weights[n], grad_w[n], given["m_" + n], given["v_" + n])
    return (loss, grad_x, *[grad_w[n] for n in TWIN_WEIGHTS], *[delta_w[n] for n in TWIN_WEIGHTS],
            *[new_m[n] for n in TWIN_WEIGHTS], *[new_v[n] for n in TWIN_WEIGHTS])
```

```python
import functools
import math

import jax
import jax.numpy as jnp
from jax import lax
from jax.experimental import pallas as pl
from jax.experimental.pallas import tpu as pltpu

F32 = jnp.float32
BF16 = jnp.bfloat16

HEAD_DIM = 128
MLA_NOPE = 128
MLA_ROPE = 64
MLA_V = 128
MLA_QK = MLA_NOPE + MLA_ROPE
MLA_QK_PAD = 256
DIL_PATTERNS = ((128, 1), (512, 4), (2048, 16))
DIL_BLK = 128
PARTIAL_ROPE = HEAD_DIM // 4
ROPE_THETA = 500000.0
RMS_EPS = 1e-6
NEG = -1e30
ADAM_LR, ADAM_B1, ADAM_B2, ADAM_EPS, ADAM_WD, ADAM_STEP = 0.001, 0.9, 0.999, 1e-08, 0.01, 10

N_CHIPS = 4
N_DEV = 8
LANES = 128
PACK_W = 1024
AG_CHUNKS = 3
VMEM_LIMIT = 56 * 1024 * 1024
MESH = pl.DeviceIdType.MESH

WEIGHT_NAMES = ('ffn1_pre_g', 'ffn1_post_g', 'ffn1_w_gate', 'ffn1_w_up', 'ffn1_w_down', 'mix_pre_g', 'mix_post_g',
                'w_in', 'mla_q_norm_g', 'mla_w_uq', 'mla_kv_norm_g', 'mla_w_ukv', 'w_o', 'ffn2_pre_g', 'ffn2_post_g',
                'ffn2_w_gate', 'ffn2_w_up', 'ffn2_w_down')
GAIN_NAMES = tuple(n for n in WEIGHT_NAMES if n.endswith('_g'))
MATRIX_NAMES = tuple(n for n in WEIGHT_NAMES if not n.endswith('_g'))
ROW_SHARDED = ('ffn1_w_down', 'ffn2_w_down', 'w_o')


def _tile(dim, pref, mult=LANES):
    if dim <= pref:
        return dim
    t = (pref // mult) * mult
    while t >= mult:
        if dim % t == 0:
            return t
        t -= mult
    return dim


def _params(*sem):
    return pltpu.CompilerParams(dimension_semantics=sem, vmem_limit_bytes=VMEM_LIMIT)


def _dot(a, b, dims):
    return lax.dot_general(a.astype(BF16), b.astype(BF16), (dims, ((), ())), preferred_element_type=F32)


_NN = ((1,), (0,))
_NT = ((1,), (1,))
_TN = ((0,), (0,))


def _mm(pairs, mode, out_dtype, name, tm=1024, tn=512, tk=512):
    a0, b0 = pairs[0]
    if mode == 'nn':
        (m, k), n = a0.shape, b0.shape[1]
    elif mode == 'nt':
        (m, k), n = a0.shape, b0.shape[0]
    else:
        (k, m), n = a0.shape, b0.shape[1]
    tm, tn, tk = _tile(m, tm), _tile(n, tn), _tile(k, tk)
    nk = k // tk
    dims = {'nn': _NN, 'nt': _NT, 'tn': _TN}[mode]
    if mode == 'tn':
        a_spec = pl.BlockSpec((tk, tm), lambda i, j, kk: (kk, i))
    else:
        a_spec = pl.BlockSpec((tm, tk), lambda i, j, kk: (i, kk))
    if mode == 'nt':
        b_spec = pl.BlockSpec((tn, tk), lambda i, j, kk: (j, kk))
    else:
        b_spec = pl.BlockSpec((tk, tn), lambda i, j, kk: (kk, j))
    n_pairs = len(pairs)

    def body(*refs):
        o_ref = refs[2 * n_pairs]

        def partial_sum():
            s = _dot(refs[0][...], refs[1][...], dims)
            for p in range(1, n_pairs):
                s = s + _dot(refs[2 * p][...], refs[2 * p + 1][...], dims)
            return s

        if nk == 1:
            o_ref[...] = partial_sum().astype(out_dtype)
        else:
            acc = refs[2 * n_pairs + 1]
            kk = pl.program_id(2)

            @pl.when(kk == 0)
            def _():
                acc[...] = jnp.zeros_like(acc)

            acc[...] += partial_sum()

            @pl.when(kk == nk - 1)
            def _():
                o_ref[...] = acc[...].astype(out_dtype)

    return pl.pallas_call(
        body, name=name, grid=(m // tm, n // tn, nk),
        in_specs=[a_spec, b_spec] * n_pairs,
        out_specs=pl.BlockSpec((tm, tn), lambda i, j, kk: (i, j)),
        out_shape=jax.ShapeDtypeStruct((m, n), out_dtype),
        scratch_shapes=[pltpu.VMEM((tm, tn), F32)] if nk > 1 else [],
        compiler_params=_params("parallel", "parallel", "arbitrary"),
    )(*[t for pair in pairs for t in pair])


def _ffn_up(h, wg, wu, name):
    m, d = h.shape
    f = wg.shape[1]
    tm, tn = _tile(m, 1024), _tile(f, 512)

    def body(h_ref, wg_ref, wu_ref, g_ref, u_ref, a_ref):
        g = _dot(h_ref[...], wg_ref[...], _NN)
        u = _dot(h_ref[...], wu_ref[...], _NN)
        g_ref[...] = g.astype(BF16)
        u_ref[...] = u.astype(BF16)
        a_ref[...] = (g * jax.nn.sigmoid(g) * u).astype(BF16)

    w_spec = pl.BlockSpec((d, tn), lambda i, j: (0, j))
    o_spec = pl.BlockSpec((tm, tn), lambda i, j: (i, j))
    shape = jax.ShapeDtypeStruct((m, f), BF16)
    return pl.pallas_call(
        body, name=name, grid=(m // tm, f // tn),
        in_specs=[pl.BlockSpec((tm, d), lambda i, j: (i, 0)), w_spec, w_spec],
        out_specs=[o_spec] * 3, out_shape=[shape] * 3,
        compiler_params=_params("parallel", "arbitrary"),
    )(h, wg, wu)


def _ffn_bwd_act(dz, wd, g, u, name):
    m, d = dz.shape
    f = wd.shape[0]
    tm, tn = _tile(m, 1024), _tile(f, 512)

    def body(dz_ref, wd_ref, g_ref, u_ref, dg_ref, du_ref):
        da = _dot(dz_ref[...], wd_ref[...], _NT)
        gg = g_ref[...].astype(F32)
        uu = u_ref[...].astype(F32)
        sg = jax.nn.sigmoid(gg)
        du_ref[...] = (da * (gg * sg)).astype(BF16)
        dg_ref[...] = (da * uu * (sg * (1.0 + gg * (1.0 - sg)))).astype(BF16)

    t_spec = pl.BlockSpec((tm, tn), lambda i, j: (i, j))
    shape = jax.ShapeDtypeStruct((m, f), BF16)
    return pl.pallas_call(
        body, name=name, grid=(m // tm, f // tn),
        in_specs=[pl.BlockSpec((tm, d), lambda i, j: (i, 0)), pl.BlockSpec((tn, d), lambda i, j: (j, 0)), t_spec, t_spec],
        out_specs=[t_spec] * 2, out_shape=[shape] * 2,
        compiler_params=_params("parallel", "arbitrary"),
    )(dz, wd, g, u)


def _rms(x, g):
    r = lax.rsqrt(jnp.mean(x * x, axis=-1, keepdims=True) + RMS_EPS)
    return x * r * g


def _rms_fwd(x, g, name, col_block=0, out_dtype=BF16):
    t = x.shape[0]
    w = g.shape[-1]
    tr = _tile(t, 256)

    def body(x_ref, g_ref, o_ref):
        o_ref[...] = _rms(x_ref[...].astype(F32), g_ref[...]).astype(out_dtype)

    return pl.pallas_call(
        body, name=name, grid=(t // tr,),
        in_specs=[pl.BlockSpec((tr, w), lambda i: (i, col_block)), pl.BlockSpec((1, w), lambda i: (0, 0))],
        out_specs=pl.BlockSpec((tr, w), lambda i: (i, 0)),
        out_shape=jax.ShapeDtypeStruct((t, w), out_dtype),
        compiler_params=_params("parallel"),
    )(x, g.reshape(1, w))


def _post_pre(x, z, g_post, alpha, g_next, name):
    t, w = x.shape
    tr = _tile(t, 256)

    def body(x_ref, z_ref, gp_ref, gn_ref, xo_ref, h_ref):
        xn = x_ref[...] + alpha * _rms(z_ref[...], gp_ref[...])
        xo_ref[...] = xn
        h_ref[...] = _rms(xn, gn_ref[...]).astype(BF16)

    row = pl.BlockSpec((tr, w), lambda i: (i, 0))
    vec = pl.BlockSpec((1, w), lambda i: (0, 0))
    return pl.pallas_call(
        body, name=name, grid=(t // tr,),
        in_specs=[row, row, vec, vec], out_specs=[row, row],
        out_shape=[jax.ShapeDtypeStruct((t, w), F32), jax.ShapeDtypeStruct((t, w), BF16)],
        compiler_params=_params("parallel"),
    )(x, z, g_post.reshape(1, w), g_next.reshape(1, w))


def _post_loss(x, z, g_post, alpha, target, name):
    t, w = x.shape
    tr = _tile(t, 256)

    def body(x_ref, z_ref, gp_ref, t_ref, dy_ref, loss_ref):
        err = x_ref[...] + alpha * _rms(z_ref[...], gp_ref[...]) - t_ref[...]
        dy_ref[...] = err * (1.0 / w)

        @pl.when(pl.program_id(0) == 0)
        def _():
            loss_ref[...] = jnp.zeros_like(loss_ref)

        loss_ref[...] += jnp.sum(jnp.mean(err * err, axis=-1, keepdims=True), axis=0, keepdims=True) * 0.5

    row = pl.BlockSpec((tr, w), lambda i: (i, 0))
    vec = pl.BlockSpec((1, w), lambda i: (0, 0))
    return pl.pallas_call(
        body, name=name, grid=(t // tr,),
        in_specs=[row, row, vec, row], out_specs=[row, pl.BlockSpec((1, 1), lambda i: (0, 0))],
        out_shape=[jax.ShapeDtypeStruct((t, w), F32), jax.ShapeDtypeStruct((1, 1), F32)],
        compiler_params=_params("arbitrary"),
    )(x, z, g_post.reshape(1, w), target)


def _rms_bwd(x, g, dy, alpha, name, res=None, col_block=0, out_dtype=F32):
    t = x.shape[0]
    w = g.shape[-1]
    tr = _tile(t, 256)
    has_res = res is not None

    def body(*refs):
        x_ref, g_ref, dy_ref = refs[:3]
        dx_ref, dg_ref = refs[-2:]
        xx = x_ref[...].astype(F32)
        dyy = dy_ref[...].astype(F32) * alpha
        r = lax.rsqrt(jnp.mean(xx * xx, axis=-1, keepdims=True) + RMS_EPS)
        xh = xx * r
        gy = dyy * g_ref[...]
        dx = r * (gy - xh * jnp.mean(gy * xh, axis=-1, keepdims=True))
        if has_res:
            dx = dx + refs[3][...]
        dx_ref[...] = dx.astype(out_dtype)

        @pl.when(pl.program_id(0) == 0)
        def _():
            dg_ref[...] = jnp.zeros_like(dg_ref)

        dg_ref[...] += jnp.sum(dyy * xh, axis=0, keepdims=True)

    row = pl.BlockSpec((tr, w), lambda i: (i, 0))
    vec = pl.BlockSpec((1, w), lambda i: (0, 0))
    ins = [x, g.reshape(1, w), dy] + ([res] if has_res else [])
    dx, dg = pl.pallas_call(
        body, name=name, grid=(t // tr,),
        in_specs=[pl.BlockSpec((tr, w), lambda i: (i, col_block)), vec, row] + ([row] if has_res else []),
        out_specs=[row, vec],
        out_shape=[jax.ShapeDtypeStruct((t, w), out_dtype), jax.ShapeDtypeStruct((1, w), F32)],
        compiler_params=_params("arbitrary"),
    )(*ins)
    return dx, dg.reshape(w)


def _rot_tables(cos, sin, period, start, fwd):
    t, s = cos.shape
    one = jnp.ones((t, start), F32)
    tail_w = period - start - 2 * s
    tail = jnp.ones((t, tail_w), F32) if start == 0 else jnp.zeros((t, tail_w), F32)
    z = lambda w: jnp.zeros((t, w), F32)
    c = jnp.concatenate([one, cos, cos, tail], axis=1)
    sm = jnp.concatenate([z(start + s), sin, z(tail_w)], axis=1)
    sp = jnp.concatenate([z(start), -sin, z(s + tail_w)], axis=1)
    if fwd:
        return c, sm, sp
    return c, jnp.roll(sp, s, axis=1), jnp.roll(sm, -s, axis=1)


def _rot_apply(x, c, sm, sp, shift):
    w = x.shape[-1]
    return x * c + pltpu.roll(x, shift, 1) * sm + pltpu.roll(x, w - shift, 1) * sp


def _rot(xs, tables, shift, name, col_block=0, width=None, out_dtype=BF16):
    t = xs[0].shape[0]
    w = width if width is not None else xs[0].shape[1]
    tr = _tile(t, 256)
    n_in = len(xs)
    period = tables[0].shape[1] if tables is not None else w

    def body(*refs):
        o_ref = refs[-1]
        for hh in range(w // period):
            sl = slice(hh * period, (hh + 1) * period)
            v = refs[0][:, sl].astype(F32)
            for p in range(1, n_in):
                v = v + refs[p][:, sl].astype(F32)
            if tables is not None:
                c_ref, sm_ref, sp_ref = refs[n_in:n_in + 3]
                v = _rot_apply(v, c_ref[...], sm_ref[...], sp_ref[...], shift)
            o_ref[:, sl] = v.astype(out_dtype)

    tab = pl.BlockSpec((tr, period), lambda i: (i, 0))
    return pl.pallas_call(
        body, name=name, grid=(t // tr,),
        in_specs=[pl.BlockSpec((tr, w), lambda i: (i, col_block))] * n_in + ([tab] * 3 if tables is not None else []),
        out_specs=pl.BlockSpec((tr, w), lambda i: (i, 0)),
        out_shape=jax.ShapeDtypeStruct((t, w), out_dtype),
        compiler_params=_params("parallel"),
    )(*xs, *(tables if tables is not None else ()))


def _mla_kfull(kv, proj, kr_block, tables, heads, name):
    t = kv.shape[0]
    tr = _tile(t, 256)

    def body(kv_ref, kr_ref, c_ref, sm_ref, sp_ref, o_ref):
        kr = _rot_apply(kr_ref[...].astype(F32), c_ref[...], sm_ref[...], sp_ref[...], MLA_ROPE // 2).astype(BF16)
        for hh in range(heads):
            o_ref[:, hh * 256:hh * 256 + 128] = kv_ref[:, hh * 256:hh * 256 + 128]
            o_ref[:, hh * 256 + 128:(hh + 1) * 256] = kr

    tab = pl.BlockSpec((tr, LANES), lambda i: (i, 0))
    full = pl.BlockSpec((tr, heads * 256), lambda i: (i, 0))
    return pl.pallas_call(
        body, name=name, grid=(t // tr,),
        in_specs=[full, pl.BlockSpec((tr, LANES), lambda i: (i, kr_block)), tab, tab, tab],
        out_specs=full, out_shape=jax.ShapeDtypeStruct((t, heads * 256), BF16),
        compiler_params=_params("parallel"),
    )(kv, proj, *tables)


def _mla_dkv(dk, dv, tables, heads, name):
    t = dk.shape[0]
    tr = _tile(t, 256)

    def body(dk_ref, dv_ref, c_ref, sm_ref, sp_ref, dkv_ref, dkr_ref):
        acc = jnp.zeros((tr, LANES), F32)
        for hh in range(heads):
            dkv_ref[:, hh * 256:hh * 256 + 128] = dk_ref[:, hh * 256:hh * 256 + 128].astype(BF16)
            dkv_ref[:, hh * 256 + 128:(hh + 1) * 256] = dv_ref[:, hh * 128:(hh + 1) * 128].astype(BF16)
            acc = acc + dk_ref[:, hh * 256 + 128:(hh + 1) * 256]
        dkr_ref[...] = _rot_apply(acc, c_ref[...], sm_ref[...], sp_ref[...], MLA_ROPE // 2).astype(BF16)

    tab = pl.BlockSpec((tr, LANES), lambda i: (i, 0))
    full = pl.BlockSpec((tr, heads * 256), lambda i: (i, 0))
    return pl.pallas_call(
        body, name=name, grid=(t // tr,),
        in_specs=[full, pl.BlockSpec((tr, heads * 128), lambda i: (i, 0)), tab, tab, tab],
        out_specs=[full, tab],
        out_shape=[jax.ShapeDtypeStruct((t, heads * 256), BF16), jax.ShapeDtypeStruct((t, LANES), BF16)],
        compiler_params=_params("parallel"),
    )(dk, dv, *tables)


def _delta(do, o, name):
    t, w = do.shape
    tr = _tile(t, 256)

    def body(do_ref, o_ref, d_ref):
        for hh in range(w // HEAD_DIM):
            sl = slice(hh * HEAD_DIM, (hh + 1) * HEAD_DIM)
            s = jnp.sum(do_ref[:, sl].astype(F32) * o_ref[:, sl].astype(F32), axis=1, keepdims=True)
            d_ref[:, sl] = jnp.broadcast_to(s, (tr, HEAD_DIM))

    row = pl.BlockSpec((tr, w), lambda i: (i, 0))
    return pl.pallas_call(
        body, name=name, grid=(t // tr,), in_specs=[row, row], out_specs=row,
        out_shape=jax.ShapeDtypeStruct((t, w), F32), compiler_params=_params("parallel"),
    )(do, o)


def _causal_mask(s, row0, col0):
    qi = row0 + lax.broadcasted_iota(jnp.int32, s.shape, 0)
    kj = col0 + lax.broadcasted_iota(jnp.int32, s.shape, 1)
    return jnp.where(kj <= qi, s, NEG)


def _flash_fwd(q, k, kv, heads, name):
    t = q.shape[0]
    tq = _tile(t, 512)
    scale = 1.0 / math.sqrt(MLA_QK)

    def body(q_ref, k_ref, v_ref, o_ref, lse_ref, m_sc, l_sc, acc_sc):
        i = pl.program_id(1)
        m_sc[...] = jnp.full_like(m_sc, NEG)
        l_sc[...] = jnp.zeros_like(l_sc)
        acc_sc[...] = jnp.zeros_like(acc_sc)

        def step(j, masked):
            kk = k_ref[pl.ds(pl.multiple_of(j * tq, tq), tq), :]
            vv = v_ref[pl.ds(pl.multiple_of(j * tq, tq), tq), :]
            s = _dot(q_ref[...], kk, _NT) * scale
            if masked:
                s = _causal_mask(s, 0, 0)
            m_prev = m_sc[:, :1]
            m_new = jnp.maximum(m_prev, jnp.max(s, axis=1, keepdims=True))
            a = jnp.exp(m_prev - m_new)
            p = jnp.exp(s - m_new)
            l_sc[...] = jnp.broadcast_to(a * l_sc[:, :1] + jnp.sum(p, axis=1, keepdims=True), l_sc.shape)
            acc_sc[...] = a * acc_sc[...] + _dot(p, vv, _NN)
            m_sc[...] = jnp.broadcast_to(m_new, m_sc.shape)

        def loop_body(j, carry):
            step(j, False)
            return carry

        lax.fori_loop(0, i, loop_body, 0)
        step(i, True)
        o_ref[...] = (acc_sc[...] / l_sc[:, :1]).astype(BF16)
        lse_ref[...] = m_sc[...] + jnp.log(l_sc[...])

    stat = pl.BlockSpec((tq, HEAD_DIM), lambda h, i: (i, h))
    return pl.pallas_call(
        body, name=name, grid=(heads, t // tq),
        in_specs=[pl.BlockSpec((tq, 256), lambda h, i: (i, h)), pl.BlockSpec((t, 256), lambda h, i: (0, h)),
                  pl.BlockSpec((t, MLA_V), lambda h, i: (0, 2 * h + 1))],
        out_specs=[stat, stat],
        out_shape=[jax.ShapeDtypeStruct((t, heads * MLA_V), BF16), jax.ShapeDtypeStruct((t, heads * HEAD_DIM), F32)],
        scratch_shapes=[pltpu.VMEM((tq, HEAD_DIM), F32)] * 3,
        compiler_params=_params("parallel", "arbitrary"),
    )(q, k, kv)


def _flash_bwd_dq(q, k, kv, do, lse, delta, heads, name):
    t = q.shape[0]
    tq = _tile(t, 512)
    scale = 1.0 / math.sqrt(MLA_QK)

    def body(q_ref, k_ref, v_ref, do_ref, lse_ref, d_ref, dq_ref):
        i = pl.program_id(1)
        dq_ref[...] = jnp.zeros_like(dq_ref)

        def step(j, masked):
            kk = k_ref[pl.ds(pl.multiple_of(j * tq, tq), tq), :]
            vv = v_ref[pl.ds(pl.multiple_of(j * tq, tq), tq), :]
            s = _dot(q_ref[...], kk, _NT) * scale
            if masked:
                s = _causal_mask(s, 0, 0)
            p = jnp.exp(s - lse_ref[:, :1])
            dp = _dot(do_ref[...], vv, _NT)
            ds = p * (dp - d_ref[:, :1]) * scale
            dq_ref[...] += _dot(ds, kk, _NN)

        def loop_body(j, carry):
            step(j, False)
            return carry

        lax.fori_loop(0, i, loop_body, 0)
        step(i, True)

    stat = pl.BlockSpec((tq, HEAD_DIM), lambda h, i: (i, h))
    qs = pl.BlockSpec((tq, 256), lambda h, i: (i, h))
    return pl.pallas_call(
        body, name=name, grid=(heads, t // tq),
        in_specs=[qs, pl.BlockSpec((t, 256), lambda h, i: (0, h)), pl.BlockSpec((t, MLA_V), lambda h, i: (0, 2 * h + 1)),
                  stat, stat, stat],
        out_specs=qs, out_shape=jax.ShapeDtypeStruct((t, heads * 256), F32),
        compiler_params=_params("parallel", "arbitrary"),
    )(q, k, kv, do, lse, delta)


def _flash_bwd_dkv(q, k, kv, do, lse, delta, heads, name):
    t = q.shape[0]
    tq = _tile(t, 512)
    nq = t // tq
    scale = 1.0 / math.sqrt(MLA_QK)

    def body(q_ref, k_ref, v_ref, do_ref, lse_ref, d_ref, dk_ref, dv_ref):
        j = pl.program_id(1)
        dk_ref[...] = jnp.zeros_like(dk_ref)
        dv_ref[...] = jnp.zeros_like(dv_ref)

        def step(i, masked):
            rows = pl.ds(pl.multiple_of(i * tq, tq), tq)
            qq = q_ref[rows, :]
            dd = do_ref[rows, :]
            s = _dot(qq, k_ref[...], _NT) * scale
            if masked:
                s = _causal_mask(s, 0, 0)
            p = jnp.exp(s - lse_ref[rows, :][:, :1])
            dv_ref[...] += _dot(p, dd, _TN)
            dp = _dot(dd, v_ref[...], _NT)
            ds = p * (dp - d_ref[rows, :][:, :1]) * scale
            dk_ref[...] += _dot(ds, qq, _TN)

        def loop_body(i, carry):
            step(i, False)
            return carry

        step(j, True)
        lax.fori_loop(j + 1, nq, loop_body, 0)

    whole = lambda w: pl.BlockSpec((t, w), lambda h, j: (0, h))
    ks = pl.BlockSpec((tq, 256), lambda h, j: (j, h))
    return pl.pallas_call(
        body, name=name, grid=(heads, nq),
        in_specs=[whole(256), ks, pl.BlockSpec((tq, MLA_V), lambda h, j: (j, 2 * h + 1)), whole(MLA_V),
                  whole(HEAD_DIM), whole(HEAD_DIM)],
        out_specs=[ks, pl.BlockSpec((tq, MLA_V), lambda h, j: (j, h))],
        out_shape=[jax.ShapeDtypeStruct((t, heads * 256), F32), jax.ShapeDtypeStruct((t, heads * MLA_V), F32)],
        compiler_params=_params("parallel", "arbitrary"),
    )(q, k, kv, do, lse, delta)


def _dil_view(a, d):
    t, w = a.shape
    return a.reshape(t // d, d * w)


def _dil_masks(first_block):
    qi = lax.broadcasted_iota(jnp.int32, (DIL_BLK, DIL_BLK), 0)
    kj = lax.broadcasted_iota(jnp.int32, (DIL_BLK, DIL_BLK), 1)
    return jnp.logical_and(kj >= qi, jnp.logical_not(first_block)), kj <= qi


def _dil_fwd(q, k, v, d, name):
    t, w = q.shape
    nb = t // d // DIL_BLK
    scale = 1.0 / math.sqrt(HEAD_DIM)

    def body(q_ref, kp_ref, kc_ref, vp_ref, vc_ref, o_ref, lse_ref):
        mask_p, mask_c = _dil_masks(pl.program_id(1) == 0)
        for hh in range(w // HEAD_DIM):
            sl = slice(hh * HEAD_DIM, (hh + 1) * HEAD_DIM)
            qh = q_ref[:, sl]
            sp = jnp.where(mask_p, _dot(qh, kp_ref[:, sl], _NT) * scale, NEG)
            sc = jnp.where(mask_c, _dot(qh, kc_ref[:, sl], _NT) * scale, NEG)
            m = jnp.maximum(jnp.max(sp, axis=1, keepdims=True), jnp.max(sc, axis=1, keepdims=True))
            pp = jnp.exp(sp - m)
            pc = jnp.exp(sc - m)
            l = jnp.sum(pp, axis=1, keepdims=True) + jnp.sum(pc, axis=1, keepdims=True)
            o_ref[:, sl] = (_dot(pp, vp_ref[:, sl], _NN) + _dot(pc, vc_ref[:, sl], _NN)) / l
            lse_ref[:, sl] = jnp.broadcast_to(m + jnp.log(l), (DIL_BLK, HEAD_DIM))

    cur = pl.BlockSpec((DIL_BLK, w), lambda r, n: (n, r))
    prev = pl.BlockSpec((DIL_BLK, w), lambda r, n: (jnp.maximum(n - 1, 0), r))
    shape = jax.ShapeDtypeStruct((t // d, d * w), F32)
    o, lse = pl.pallas_call(
        body, name=name, grid=(d, nb),
        in_specs=[cur, prev, cur, prev, cur], out_specs=[cur, cur], out_shape=[shape, shape],
        compiler_params=_params("parallel", "arbitrary"),
    )(_dil_view(q, d), _dil_view(k, d), _dil_view(k, d), _dil_view(v, d), _dil_view(v, d))
    return o.reshape(t, w), lse.reshape(t, w)


def _dil_combine(os_, lses, name):
    t, w = os_[0].shape
    tr = _tile(t, 256)
    n = len(os_)

    def body(*refs):
        o_ref, m_ref = refs[-2:]
        ls = [refs[n + p][...] for p in range(n)]
        m = functools.reduce(jnp.maximum, ls)
        es = [jnp.exp(l - m) for l in ls]
        den = functools.reduce(jnp.add, es)
        acc = es[0] / den * refs[0][...]
        for p in range(1, n):
            acc = acc + es[p] / den * refs[p][...]
        o_ref[...] = acc.astype(BF16)
        m_ref[...] = m + jnp.log(den)

    row = pl.BlockSpec((tr, w), lambda i: (i, 0))
    return pl.pallas_call(
        body, name=name, grid=(t // tr,), in_specs=[row] * (2 * n), out_specs=[row, row],
        out_shape=[jax.ShapeDtypeStruct((t, w), BF16), jax.ShapeDtypeStruct((t, w), F32)],
        compiler_params=_params("parallel"),
    )(*os_, *lses)


def _dil_bwd_dq(q, k, v, do, mt, delta, d, name):
    t, w = q.shape
    nb = t // d // DIL_BLK
    scale = 1.0 / math.sqrt(HEAD_DIM)

    def body(q_ref, kp_ref, kc_ref, vp_ref, vc_ref, do_ref, mt_ref, d_ref, dq_ref):
        mask_p, mask_c = _dil_masks(pl.program_id(1) == 0)
        for hh in range(w // HEAD_DIM):
            sl = slice(hh * HEAD_DIM, (hh + 1) * HEAD_DIM)
            qh = q_ref[:, sl]
            dd = do_ref[:, sl]
            mrow = mt_ref[:, hh * HEAD_DIM:hh * HEAD_DIM + 1]
            drow = d_ref[:, hh * HEAD_DIM:hh * HEAD_DIM + 1]
            sp = jnp.where(mask_p, _dot(qh, kp_ref[:, sl], _NT) * scale, NEG)
            sc = jnp.where(mask_c, _dot(qh, kc_ref[:, sl], _NT) * scale, NEG)
            dsp = jnp.exp(sp - mrow) * (_dot(dd, vp_ref[:, sl], _NT) - drow) * scale
            dsc = jnp.exp(sc - mrow) * (_dot(dd, vc_ref[:, sl], _NT) - drow) * scale
            dq_ref[:, sl] = _dot(dsp, kp_ref[:, sl], _NN) + _dot(dsc, kc_ref[:, sl], _NN)

    cur = pl.BlockSpec((DIL_BLK, w), lambda r, n: (n, r))
    prev = pl.BlockSpec((DIL_BLK, w), lambda r, n: (jnp.maximum(n - 1, 0), r))
    kv_, vv_ = _dil_view(k, d), _dil_view(v, d)
    dq = pl.pallas_call(
        body, name=name, grid=(d, nb),
        in_specs=[cur, prev, cur, prev, cur, cur, cur, cur], out_specs=cur,
        out_shape=jax.ShapeDtypeStruct((t // d, d * w), F32),
        compiler_params=_params("parallel", "arbitrary"),
    )(_dil_view(q, d), kv_, kv_, vv_, vv_, _dil_view(do, d), _dil_view(mt, d), _dil_view(delta, d))
    return dq.reshape(t, w)


def _dil_bwd_dkv(q, k, v, do, mt, delta, d, name):
    t, w = q.shape
    nb = t // d // DIL_BLK
    scale = 1.0 / math.sqrt(HEAD_DIM)

    def body(k_ref, v_ref, qs_ref, qn_ref, dos_ref, don_ref, ms_ref, mn_ref, ds_ref, dn_ref, dk_ref, dv_ref):
        last = pl.program_id(1) == nb - 1
        qi = lax.broadcasted_iota(jnp.int32, (DIL_BLK, DIL_BLK), 0)
        kj = lax.broadcasted_iota(jnp.int32, (DIL_BLK, DIL_BLK), 1)
        mask_s = kj <= qi
        mask_n = jnp.logical_and(kj >= qi, jnp.logical_not(last))
        for hh in range(w // HEAD_DIM):
            sl = slice(hh * HEAD_DIM, (hh + 1) * HEAD_DIM)
            st = slice(hh * HEAD_DIM, hh * HEAD_DIM + 1)
            kh, vh = k_ref[:, sl], v_ref[:, sl]
            dk = jnp.zeros((DIL_BLK, HEAD_DIM), F32)
            dv = jnp.zeros((DIL_BLK, HEAD_DIM), F32)
            for q_ref, do_ref, m_ref, d_ref, mask in ((qs_ref, dos_ref, ms_ref, ds_ref, mask_s),
                                                      (qn_ref, don_ref, mn_ref, dn_ref, mask_n)):
                qh, dd = q_ref[:, sl], do_ref[:, sl]
                s = jnp.where(mask, _dot(qh, kh, _NT) * scale, NEG)
                p = jnp.exp(s - m_ref[:, st])
                dv = dv + _dot(p, dd, _TN)
                dsv = p * (_dot(dd, vh, _NT) - d_ref[:, st]) * scale
                dk = dk + _dot(dsv, qh, _TN)
            dk_ref[:, sl] = dk
            dv_ref[:, sl] = dv

    cur = pl.BlockSpec((DIL_BLK, w), lambda r, n: (n, r))
    nxt = pl.BlockSpec((DIL_BLK, w), lambda r, n: (jnp.minimum(n + 1, nb - 1), r))
    shape = jax.ShapeDtypeStruct((t // d, d * w), F32)
    qv, dov, mv, dv_ = _dil_view(q, d), _dil_view(do, d), _dil_view(mt, d), _dil_view(delta, d)
    dk, dv = pl.pallas_call(
        body, name=name, grid=(d, nb),
        in_specs=[cur, cur, cur, nxt, cur, nxt, cur, nxt, cur, nxt], out_specs=[cur, cur], out_shape=[shape, shape],
        compiler_params=_params("parallel", "arbitrary"),
    )(_dil_view(k, d), _dil_view(v, d), qv, qv, dov, dov, mv, mv, dv_, dv_)
    return dk.reshape(t, w), dv.reshape(t, w)


def _adamw(w, g, m, v, name):
    r, c = w.shape
    tr = _tile(r, 256, 8)
    c1 = 1.0 - ADAM_B1 ** ADAM_STEP
    c2 = 1.0 - ADAM_B2 ** ADAM_STEP

    def body(w_ref, g_ref, m_ref, v_ref, d_ref, mo_ref, vo_ref):
        gg = g_ref[...]
        mn = ADAM_B1 * m_ref[...] + (1.0 - ADAM_B1) * gg
        vn = ADAM_B2 * v_ref[...] + (1.0 - ADAM_B2) * (gg * gg)
        mo_ref[...] = mn
        vo_ref[...] = vn
        d_ref[...] = -ADAM_LR * ((mn / c1) / (jnp.sqrt(vn / c2) + ADAM_EPS) + ADAM_WD * w_ref[...])

    row = pl.BlockSpec((tr, c), lambda i: (i, 0))
    shape = jax.ShapeDtypeStruct((r, c), F32)
    return pl.pallas_call(
        body, name=name, grid=(r // tr,), in_specs=[row] * 4, out_specs=[row] * 3, out_shape=[shape] * 3,
        compiler_params=_params("parallel"),
    )(w, g, m, v)


def _position():
    return lax.axis_index("x"), lax.axis_index("y"), lax.axis_index("c")


def _other_chips(x, y):
    return [(1 - x, y), (x, 1 - y), (1 - x, 1 - y)]


_ANY = pl.BlockSpec(memory_space=pl.ANY)


def _allgather_shards(wp, name):
    r, w = wp.shape
    rh = r // 2
    rc = rh // AG_CHUNKS

    def body(w_ref, out_ref, send_sems, recv_sems, local_sem):
        x, y, c = _position()
        me = 2 * x + y
        sibling = (x, y, 1 - c)
        chips = _other_chips(x, y)

        def rows(half, q):
            return pl.ds(pl.multiple_of(half * rh + q * rc, 16), rc)

        def copy(level, kk, q, src, dst, to):
            return pltpu.make_async_remote_copy(src_ref=src, dst_ref=dst, send_sem=send_sems.at[level, kk, q],
                                                recv_sem=recv_sems.at[level, kk, q], device_id=to, device_id_type=MESH)

        mine = pltpu.make_async_copy(w_ref, out_ref.at[me], local_sem)
        mine.start()
        started = []
        for kk, (px, py) in enumerate(chips):
            for q in range(AG_CHUNKS):
                cp = copy(0, kk, q, w_ref.at[rows(c, q)], out_ref.at[me, rows(c, q)], (px, py, c))
                cp.start()
                started.append(cp)
        for kk, (px, py) in enumerate(chips):
            for q in range(AG_CHUNKS):
                landed = out_ref.at[2 * px + py, rows(c, q)]
                copy(0, kk, q, landed, landed, (px, py, c)).wait_recv()
                cp = copy(1, kk, q, landed, landed, sibling)
                cp.start()
                started.append(cp)
        for kk, (px, py) in enumerate(chips):
            for q in range(AG_CHUNKS):
                passed = out_ref.at[2 * px + py, rows(1 - c, q)]
                copy(1, kk, q, passed, passed, sibling).wait_recv()
        for cp in started:
            cp.wait_send()
        mine.wait()

    return pl.pallas_call(
        body, name=name, in_specs=[_ANY], out_specs=_ANY,
        out_shape=jax.ShapeDtypeStruct((N_CHIPS, r, w), wp.dtype),
        scratch_shapes=[pltpu.SemaphoreType.DMA((2, 3, AG_CHUNKS)), pltpu.SemaphoreType.DMA((2, 3, AG_CHUNKS)),
                        pltpu.SemaphoreType.DMA],
    )(wp)


def _swap_halves(gp, name):
    n, r, w = gp.shape
    rh = r // 2

    def body(g_ref, out_ref, send_sems, recv_sems):
        x, y, c = _position()
        cps = []
        for j in range(n):
            cp = pltpu.make_async_remote_copy(
                src_ref=g_ref.at[j, pl.ds(pl.multiple_of((1 - c) * rh, 16), rh)], dst_ref=out_ref.at[j],
                send_sem=send_sems.at[j], recv_sem=recv_sems.at[j], device_id=(x, y, 1 - c), device_id_type=MESH)
            cp.start()
            cps.append(cp)
        for cp in cps:
            cp.wait_recv()
        for cp in cps:
            cp.wait_send()

    return pl.pallas_call(
        body, name=name, in_specs=[_ANY], out_specs=_ANY,
        out_shape=jax.ShapeDtypeStruct((n, rh, w), gp.dtype),
        scratch_shapes=[pltpu.SemaphoreType.DMA((n,)), pltpu.SemaphoreType.DMA((n,))],
    )(gp)


def _exchange_chips(sums, name):
    n, rh, w = sums.shape

    def body(s_ref, out_ref, send_sems, recv_sems):
        x, y, c = _position()
        cps = []
        for kk, (px, py) in enumerate(_other_chips(x, y)):
            cp = pltpu.make_async_remote_copy(
                src_ref=s_ref.at[2 * px + py], dst_ref=out_ref.at[kk], send_sem=send_sems.at[kk],
                recv_sem=recv_sems.at[kk], device_id=(px, py, c), device_id_type=MESH)
            cp.start()
            cps.append(cp)
        for cp in cps:
            cp.wait_recv()
        for cp in cps:
            cp.wait_send()

    return pl.pallas_call(
        body, name=name, in_specs=[_ANY], out_specs=_ANY,
        out_shape=jax.ShapeDtypeStruct((3, rh, w), sums.dtype),
        scratch_shapes=[pltpu.SemaphoreType.DMA((3,)), pltpu.SemaphoreType.DMA((3,))],
    )(sums)


def _share_halves(half, name):
    rh, w = half.shape

    def body(h_ref, out_ref, send_sem, recv_sem, local_sem):
        x, y, c = _position()
        mine = pltpu.make_async_copy(h_ref, out_ref.at[c], local_sem)
        mine.start()
        cp = pltpu.make_async_remote_copy(src_ref=h_ref, dst_ref=out_ref.at[c], send_sem=send_sem, recv_sem=recv_sem,
                                          device_id=(x, y, 1 - c), device_id_type=MESH)
        cp.start()
        pltpu.make_async_remote_copy(src_ref=h_ref, dst_ref=out_ref.at[1 - c], send_sem=send_sem, recv_sem=recv_sem,
                                     device_id=(x, y, 1 - c), device_id_type=MESH).wait_recv()
        cp.wait_send()
        mine.wait()

    return pl.pallas_call(
        body, name=name, in_specs=[_ANY], out_specs=_ANY,
        out_shape=jax.ShapeDtypeStruct((2, rh, w), half.dtype),
        scratch_shapes=[pltpu.SemaphoreType.DMA, pltpu.SemaphoreType.DMA, pltpu.SemaphoreType.DMA],
    )(half)


def _add_pair(gp, got, name):
    n, _, rh, w = gp.shape
    tr = _tile(rh, 512, 16)

    def body(c_ref, a_ref, b_ref, o_ref):
        o_ref[...] = (a_ref[...].astype(F32) + b_ref[...].astype(F32)).astype(BF16)

    blk = pl.BlockSpec((None, tr, w), lambda j, i, c_ref: (j, i, 0))
    return pl.pallas_call(
        body, name=name,
        grid_spec=pltpu.PrefetchScalarGridSpec(
            num_scalar_prefetch=1, grid=(n, rh // tr),
            in_specs=[pl.BlockSpec((None, None, tr, w), lambda j, i, c_ref: (j, c_ref[0], i, 0)), blk], out_specs=blk),
        out_shape=jax.ShapeDtypeStruct((n, rh, w), BF16),
        compiler_params=_params("parallel", "parallel"),
    )(lax.axis_index("c").reshape(1).astype(jnp.int32), gp, got)


def _add_chips(sums, got, name):
    n, rh, w = sums.shape
    tr = _tile(rh, 512, 16)

    def body(me_ref, a_ref, b_ref, o_ref):
        o_ref[...] = ((a_ref[...].astype(F32) + b_ref[0].astype(F32)) + b_ref[1].astype(F32)) + b_ref[2].astype(F32)

    me = (2 * lax.axis_index("x") + lax.axis_index("y")).reshape(1).astype(jnp.int32)
    return pl.pallas_call(
        body, name=name,
        grid_spec=pltpu.PrefetchScalarGridSpec(
            num_scalar_prefetch=1, grid=(rh // tr,),
            in_specs=[pl.BlockSpec((None, tr, w), lambda i, me_ref: (me_ref[0], i, 0)),
                      pl.BlockSpec((3, tr, w), lambda i, me_ref: (0, i, 0))],
            out_specs=pl.BlockSpec((tr, w), lambda i, me_ref: (i, 0))),
        out_shape=jax.ShapeDtypeStruct((rh, w), F32),
        compiler_params=_params("parallel"),
    )(me, sums, got)


def _allgather_small(v, name):
    m_per, w = v.shape

    def body(x_ref, out_ref, send_sems, recv_sems, local_sem):
        x, y, c = _position()
        me, sibling = (x, y, c), (x, y, 1 - c)
        chips = _other_chips(x, y)

        def rows(px, py, pc):
            return out_ref.at[pl.ds(pl.multiple_of((4 * px + 2 * py + pc) * m_per, 8), m_per), :]

        def copy(kk, block, to, src=None):
            return pltpu.make_async_remote_copy(
                src_ref=rows(*block) if src is None else src, dst_ref=rows(*block), send_sem=send_sems.at[kk],
                recv_sem=recv_sems.at[kk], device_id=to, device_id_type=MESH)

        mine = pltpu.make_async_copy(x_ref, rows(*me), local_sem)
        mine.start()
        first = [copy(0, me, sibling, src=x_ref)]
        first += [copy(1 + j, me, (*chip, c), src=x_ref) for j, chip in enumerate(chips)]
        for cp in first:
            cp.start()
        passed = [copy(4 + j, (*chip, c), sibling) for j, chip in enumerate(chips)]
        for j, chip in enumerate(chips):
            copy(1 + j, (*chip, c), me).wait_recv()
            passed[j].start()
        copy(0, sibling, me).wait_recv()
        for j, chip in enumerate(chips):
            copy(4 + j, (*chip, 1 - c), me).wait_recv()
        for cp in first + passed:
            cp.wait_send()
        mine.wait()

    return pl.pallas_call(
        body, name=name,
        out_shape=jax.ShapeDtypeStruct((N_DEV * m_per, w), v.dtype),
        in_specs=[pl.BlockSpec(memory_space=pltpu.VMEM)], out_specs=pl.BlockSpec(memory_space=pltpu.VMEM),
        scratch_shapes=[pltpu.SemaphoreType.DMA((7,)), pltpu.SemaphoreType.DMA((7,)), pltpu.SemaphoreType.DMA],
    )(v)


def _sum_blocks(a, n, name):
    m_per, w = a.shape[0] // n, a.shape[1]

    def body(a_ref, o_ref):
        s = a_ref[0:m_per, :]
        for j in range(1, n):
            s = s + a_ref[j * m_per:(j + 1) * m_per, :]
        o_ref[...] = s

    return pl.pallas_call(body, name=name, out_shape=jax.ShapeDtypeStruct((m_per, w), a.dtype))(a)


def _pad_rows(flat, lead, row_mult):
    n = flat.shape[-1]
    rows = -(-n // (PACK_W * row_mult)) * row_mult
    flat = jnp.pad(flat, ((0, 0), (0, rows * PACK_W - n)))
    return flat.reshape(lead, rows, PACK_W)


def _pack_shard(ws, layer):
    flat = jnp.concatenate([ws[n][layer].astype(BF16).reshape(1, -1) for n in MATRIX_NAMES], axis=1)
    return _pad_rows(flat, 1, 32 * AG_CHUNKS)[0]


def _unpack_full(slabs, shard_shapes):
    flat = slabs.reshape(N_CHIPS, -1)
    out, off = {}, 0
    for n in MATRIX_NAMES:
        a, b = shard_shapes[n]
        sh = flat[:, off:off + a * b].reshape(N_CHIPS, a, b)
        off += a * b
        out[n] = sh.reshape(N_CHIPS * a, b) if n in ROW_SHARDED else jnp.transpose(sh, (1, 0, 2)).reshape(a, N_CHIPS * b)
    return out


def _pack_grads(grads, shard_shapes):
    parts = []
    for n in MATRIX_NAMES:
        a, b = shard_shapes[n]
        g = grads[n]
        sh = g.reshape(N_CHIPS, a, b) if n in ROW_SHARDED else jnp.transpose(g.reshape(a, N_CHIPS, b), (1, 0, 2))
        parts.append(sh.reshape(N_CHIPS, a * b).astype(BF16))
    return _pad_rows(jnp.concatenate(parts, axis=1), N_CHIPS, 32 * AG_CHUNKS)


def _unpack_shard(slab, shard_shapes):
    flat = slab.reshape(-1)
    out, off = {}, 0
    for n in MATRIX_NAMES:
        a, b = shard_shapes[n]
        out[n] = flat[off:off + a * b].reshape(a, b)
        off += a * b
    return out


def _pack_gains(d):
    flat = jnp.concatenate([d[n].reshape(1, -1) for n in GAIN_NAMES], axis=1)
    return _pad_rows(flat, 1, 8)[0]


def _unpack_gains(slab, shapes):
    flat = slab.reshape(-1)
    out, off = {}, 0
    for n in GAIN_NAMES:
        a, b = shapes[n]
        out[n] = flat[off:off + a * b].reshape(a, b)
        off += a * b
    return out


def _reduce_scatter(gp, tag):
    n, r, w = gp.shape
    got = _swap_halves(gp, "rs_swap_halves" + tag)
    sums = _add_pair(gp.reshape(n, 2, r // 2, w), got, "rs_add_pair" + tag)
    got = _exchange_chips(sums, "rs_exchange_chips" + tag)
    half = _add_chips(sums, got, "rs_add_chips" + tag)
    return _share_halves(half, "rs_share_halves" + tag).reshape(r, w)


def _ffn_forward(h, wl, pre, tag):
    g, u, a = _ffn_up(h, wl[pre + '_w_gate'], wl[pre + '_w_up'], "ffn_up" + tag)
    z = _mm([(a, wl[pre + '_w_down'])], 'nn', F32, "ffn_down" + tag)
    return z, (g, u, a)


def _ffn_backward(dz, h, saved, wl, pre, grads, tag):
    g, u, a = saved
    dg, du = _ffn_bwd_act(dz, wl[pre + '_w_down'], g, u, "ffn_bwd_act" + tag)
    grads[pre + '_w_down'] = _mm([(a, dz)], 'tn', F32, "ffn_dw_down" + tag, tm=1408, tn=1024, tk=1024)
    grads[pre + '_w_gate'] = _mm([(h, dg)], 'tn', F32, "ffn_dw_gate" + tag, tm=1024, tn=1408, tk=1024)
    grads[pre + '_w_up'] = _mm([(h, du)], 'tn', F32, "ffn_dw_up" + tag, tm=1024, tn=1408, tk=1024)
    return _mm([(dg, wl[pre + '_w_gate']), (du, wl[pre + '_w_up'])], 'nt', F32, "ffn_dh" + tag, tn=1024)


def kernel(x, positions, ffn1_pre_g, ffn1_post_g, ffn1_w_gate, ffn1_w_up, ffn1_w_down, mix_pre_g, mix_post_g, w_in, mla_q_norm_g, mla_w_uq, mla_kv_norm_g, mla_w_ukv, w_o, ffn2_pre_g, ffn2_post_g, ffn2_w_gate, ffn2_w_up, ffn2_w_down, loss_target, m_ffn1_pre_g, m_ffn1_post_g, m_ffn1_w_gate, m_ffn1_w_up, m_ffn1_w_down, m_mix_pre_g, m_mix_post_g, m_w_in, m_mla_q_norm_g, m_mla_w_uq, m_mla_kv_norm_g, m_mla_w_ukv, m_w_o, m_ffn2_pre_g, m_ffn2_post_g, m_ffn2_w_gate, m_ffn2_w_up, m_ffn2_w_down, v_ffn1_pre_g, v_ffn1_post_g, v_ffn1_w_gate, v_ffn1_w_up, v_ffn1_w_down, v_mix_pre_g, v_mix_post_g, v_w_in, v_mla_q_norm_g, v_mla_w_uq, v_mla_kv_norm_g, v_mla_w_ukv, v_w_o, v_ffn2_pre_g, v_ffn2_post_g, v_ffn2_w_gate, v_ffn2_w_up, v_ffn2_w_down):
    given = dict(locals())
    ws = {n: given[n] for n in WEIGHT_NAMES}
    ms = {n: given['m_' + n] for n in WEIGHT_NAMES}
    vs = {n: given['v_' + n] for n in WEIGHT_NAMES}

    depth = ffn1_pre_g.shape[0]
    t, d_model = x.shape[1], x.shape[2]
    rq, rkv = mla_q_norm_g.shape[1], mla_kv_norm_g.shape[1]
    h_mla = mla_w_uq.shape[2] * N_CHIPS // MLA_QK
    wd_ = (w_in.shape[2] * N_CHIPS - rq - rkv - MLA_ROPE) // 3
    assert rq == rkv and (rq + rkv) % wd_ == 0 and wd_ % HEAD_DIM == 0
    assert mla_w_ukv.shape[2] * N_CHIPS == h_mla * 256 and w_o.shape[1] * N_CHIPS == h_mla * MLA_V + wd_
    lat = rq + rkv
    in_p = lat + 3 * wd_ + LANES
    qd_block = lat // wd_
    kr_block = (lat + 3 * wd_) // LANES
    shard_shapes = {n: ws[n].shape[1:] for n in MATRIX_NAMES}
    gain_shapes = {n: ws[n].shape for n in GAIN_NAMES}

    pos = positions[0].astype(F32)[:, None]

    def cos_sin(dim):
        inv = ROPE_THETA ** (-jnp.arange(0, dim, 2, dtype=F32) / dim)
        return jnp.cos(pos * inv), jnp.sin(pos * inv)

    cos_a, sin_a = cos_sin(MLA_ROPE)
    cos_p, sin_p = cos_sin(PARTIAL_ROPE)
    tab_q = [_rot_tables(cos_a, sin_a, MLA_QK_PAD, MLA_NOPE, f) for f in (True, False)]
    tab_kr = [_rot_tables(cos_a, sin_a, LANES, 0, f) for f in (True, False)]
    tab_d = [_rot_tables(cos_p, sin_p, HEAD_DIM, 0, f) for f in (True, False)]
    s_a, s_p = MLA_ROPE // 2, PARTIAL_ROPE // 2

    layers = []
    for l in range(depth):
        wl = _unpack_full(_allgather_shards(_pack_shard(ws, l), "allgather_weights"), shard_shapes)
        w_in_l = wl['w_in']
        wl['w_in'] = jnp.concatenate([w_in_l[:, :lat], w_in_l[:, lat + MLA_ROPE:], w_in_l[:, lat:lat + MLA_ROPE],
                                      jnp.zeros((d_model, LANES - MLA_ROPE), BF16)], axis=1)
        wl['mla_w_uq'] = jnp.pad(wl['mla_w_uq'].reshape(rq, h_mla, MLA_QK),
                                 ((0, 0), (0, 0), (0, MLA_QK_PAD - MLA_QK))).reshape(rq, h_mla * MLA_QK_PAD)
        layers.append(wl)

    xc = x[0]
    h = _rms_fwd(xc, ffn1_pre_g[0], "rms_first")
    saved = []
    dy = loss = None
    for l in range(depth):
        wl, sv = layers[l], {}
        sv['x0'], sv['h1'] = xc, h
        z, sv['ffn1'] = _ffn_forward(h, wl, 'ffn1', "_1")
        sv['z1'] = z
        xc, h = _post_pre(xc, z, ffn1_post_g[l], 0.5, mix_pre_g[l], "post_pre_1")
        sv['x1'], sv['h2'] = xc, h

        proj = _mm([(h, wl['w_in'])], 'nn', BF16, "mix_in", tn=1408, tk=d_model)
        cqn = _rms_fwd(proj, mla_q_norm_g[l], "rms_cq", col_block=0)
        ckvn = _rms_fwd(proj, mla_kv_norm_g[l], "rms_ckv", col_block=1)
        qf = _rot([_mm([(cqn, wl['mla_w_uq'])], 'nn', F32, "mla_uq")], tab_q[0], s_a, "rope_q")
        kv = _mm([(ckvn, wl['mla_w_ukv'])], 'nn', BF16, "mla_ukv")
        kf = _mla_kfull(kv, proj, kr_block, tab_kr[0], h_mla, "mla_kfull")
        oa, lse = _flash_fwd(qf, kf, kv, h_mla, "mla_attn")
        qd = _rot([proj], tab_d[0], s_p, "rope_qd", col_block=qd_block, width=wd_)
        kd = _rot([proj], tab_d[0], s_p, "rope_kd", col_block=qd_block + 1, width=wd_)
        vd = proj[:, lat + 2 * wd_:lat + 3 * wd_]
        outs = [_dil_fwd(qd, kd, vd, dil, "dil_attn_%d" % dil) for _, dil in DIL_PATTERNS]
        ob, mt = _dil_combine([o for o, _ in outs], [s for _, s in outs], "dil_combine")
        o = jnp.concatenate([oa, ob], axis=1)
        z = _mm([(o, wl['w_o'])], 'nn', F32, "mix_out")
        sv.update(proj=proj, cqn=cqn, ckvn=ckvn, qf=qf, kv=kv, kf=kf, oa=oa, lse=lse, qd=qd, kd=kd, vd=vd, ob=ob,
                  mt=mt, o=o, z2=z)
        xc, h = _post_pre(xc, z, mix_post_g[l], 1.0, ffn2_pre_g[l], "post_pre_2")
        sv['x2'], sv['h3'] = xc, h

        z, sv['ffn2'] = _ffn_forward(h, wl, 'ffn2', "_2")
        sv['z3'] = z
        if l + 1 < depth:
            xc, h = _post_pre(xc, z, ffn2_post_g[l], 0.5, ffn1_pre_g[l + 1], "post_pre_3")
        else:
            dy, loss = _post_loss(xc, z, ffn2_post_g[l], 0.5, loss_target[0], "post_loss")
        saved.append(sv)

    dx = dy
    gain_grads = {n: [None] * depth for n in GAIN_NAMES}
    shard_grads = [None] * depth
    for l in reversed(range(depth)):
        wl, sv, grads = layers[l], saved[l], {}

        dz, gain_grads['ffn2_post_g'][l] = _rms_bwd(sv['z3'], ffn2_post_g[l], dx, 0.5, "rms_bwd_post_f", out_dtype=BF16)
        dh = _ffn_backward(dz, sv['h3'], sv['ffn2'], wl, 'ffn2', grads, "_2")
        dx, gain_grads['ffn2_pre_g'][l] = _rms_bwd(sv['x2'], ffn2_pre_g[l], dh, 1.0, "rms_bwd_pre", res=dx)

        dz, gain_grads['mix_post_g'][l] = _rms_bwd(sv['z2'], mix_post_g[l], dx, 1.0, "rms_bwd_post_m", out_dtype=BF16)
        do = _mm([(dz, wl['w_o'])], 'nt', BF16, "mix_out_dx", tk=d_model)
        grads['w_o'] = _mm([(sv['o'], dz)], 'tn', F32, "mix_out_dw", tn=1024, tk=1024)
        do_a, do_b = do[:, :h_mla * MLA_V], do[:, h_mla * MLA_V:]

        delta = _delta(do_a, sv['oa'], "delta_a")
        dqf = _flash_bwd_dq(sv['qf'], sv['kf'], sv['kv'], do_a, sv['lse'], delta, h_mla, "mla_attn_dq")
        dkf, dva = _flash_bwd_dkv(sv['qf'], sv['kf'], sv['kv'], do_a, sv['lse'], delta, h_mla, "mla_attn_dkv")
        dq = _rot([dqf], tab_q[1], s_a, "rope_q_bwd")
        grads['mla_w_uq'] = _mm([(sv['cqn'], dq)], 'tn', F32, "mla_uq_dw", tk=1024)
        dcqn = _mm([(dq, wl['mla_w_uq'])], 'nt', F32, "mla_uq_dx", tk=1024)
        dcq, gain_grads['mla_q_norm_g'][l] = _rms_bwd(sv['proj'], mla_q_norm_g[l], dcqn, 1.0, "rms_bwd_cq",
                                                      col_block=0, out_dtype=BF16)
        dkv, dkr = _mla_dkv(dkf, dva, tab_kr[1], h_mla, "mla_dkv")
        grads['mla_w_ukv'] = _mm([(sv['ckvn'], dkv)], 'tn', F32, "mla_ukv_dw", tk=1024)
        dckvn = _mm([(dkv, wl['mla_w_ukv'])], 'nt', F32, "mla_ukv_dx", tk=1024)
        dckv, gain_grads['mla_kv_norm_g'][l] = _rms_bwd(sv['proj'], mla_kv_norm_g[l], dckvn, 1.0, "rms_bwd_ckv",
                                                        col_block=1, out_dtype=BF16)

        delta = _delta(do_b, sv['ob'], "delta_b")
        dqs, dks, dvs = [], [], []
        for _, dil in DIL_PATTERNS:
            dqs.append(_dil_bwd_dq(sv['qd'], sv['kd'], sv['vd'], do_b, sv['mt'], delta, dil, "dil_attn_dq_%d" % dil))
            dk_, dv_ = _dil_bwd_dkv(sv['qd'], sv['kd'], sv['vd'], do_b, sv['mt'], delta, dil, "dil_attn_dkv_%d" % dil)
            dks.append(dk_)
            dvs.append(dv_)
        dqd = _rot(dqs, tab_d[1], s_p, "rope_qd_bwd")
        dkd = _rot(dks, tab_d[1], s_p, "rope_kd_bwd")
        dvd = _rot(dvs, None, 0, "sum_dvd")
        dproj = jnp.concatenate([dcq, dckv, dqd, dkd, dvd, dkr], axis=1)
        grads['w_in'] = _mm([(sv['h2'], dproj)], 'tn', F32, "mix_in_dw", tn=1408, tk=1024)
        dh = _mm([(dproj, wl['w_in'])], 'nt', F32, "mix_in_dx", tn=1024, tk=1408)
        dx, gain_grads['mix_pre_g'][l] = _rms_bwd(sv['x1'], mix_pre_g[l], dh, 1.0, "rms_bwd_pre", res=dx)

        dz, gain_grads['ffn1_post_g'][l] = _rms_bwd(sv['z1'], ffn1_post_g[l], dx, 0.5, "rms_bwd_post_f", out_dtype=BF16)
        dh = _ffn_backward(dz, sv['h1'], sv['ffn1'], wl, 'ffn1', grads, "_1")
        dx, gain_grads['ffn1_pre_g'][l] = _rms_bwd(sv['x0'], ffn1_pre_g[l], dh, 1.0, "rms_bwd_pre", res=dx)

        gp = grads['w_in']
        grads['w_in'] = jnp.concatenate([gp[:, :lat], gp[:, in_p - LANES:in_p - LANES + MLA_ROPE], gp[:, lat:lat + 3 * wd_]],
                                        axis=1)
        grads['mla_w_uq'] = grads['mla_w_uq'].reshape(rq, h_mla, MLA_QK_PAD)[:, :, :MLA_QK].reshape(rq, h_mla * MLA_QK)
        slab = _reduce_scatter(_pack_grads(grads, shard_shapes), "")
        shard_grads[l] = _unpack_shard(slab, shard_shapes)

    gg = _pack_gains({n: jnp.stack(gain_grads[n]) for n in GAIN_NAMES})
    gg = _sum_blocks(_allgather_small(gg, "allgather_gain_grads"), N_DEV, "sum_gain_grads")

    grad_w, delta_w, new_m, new_v = {}, {}, {}, {}
    dg, mg, vg = _adamw(_pack_gains(ws), gg, _pack_gains(ms), _pack_gains(vs), "adamw_gains")
    for dst, slab in ((grad_w, gg), (delta_w, dg), (new_m, mg), (new_v, vg)):
        dst.update(_unpack_gains(slab, gain_shapes))
    for n in MATRIX_NAMES:
        shape = ws[n].shape
        g = jnp.stack([shard_grads[l][n] for l in range(depth)])
        flat = lambda a: a.reshape(shape[0] * shape[1], shape[2])
        d_, m_, v_ = _adamw(flat(ws[n]), flat(g), flat(ms[n]), flat(vs[n]), "adamw_" + n)
        grad_w[n], delta_w[n], new_m[n], new_v[n] = g, d_.reshape(shape), m_.reshape(shape), v_.reshape(shape)

    loss = lax.psum(loss[0, 0], ("x", "y", "c"))
    return (loss, dx[None], *[grad_w[n] for n in WEIGHT_NAMES], *[delta_w[n] for n in WEIGHT_NAMES],
            *[new_m[n] for n in WEIGHT_NAMES], *[new_v[n] for n in WEIGHT_NAMES])
```

```python
import functools
import math

import jax
import jax.numpy as jnp
from jax import lax
from jax.experimental import pallas as pl
from jax.experimental.pallas import tpu as pltpu

F32 = jnp.float32
BF16 = jnp.bfloat16

HEAD_DIM = 128
MLA_NOPE = 128
MLA_ROPE = 64
MLA_V = 128
MLA_QK = MLA_NOPE + MLA_ROPE
MLA_QK_PAD = 256
DIL_PATTERNS = ((128, 1), (512, 4), (2048, 16))
DIL_BLK = 128
PARTIAL_ROPE = HEAD_DIM // 4
ROPE_THETA = 500000.0
RMS_EPS = 1e-6
NEG = -1e30
ADAM_LR, ADAM_B1, ADAM_B2, ADAM_EPS, ADAM_WD, ADAM_STEP = 0.001, 0.9, 0.999, 1e-08, 0.01, 10

N_CHIPS = 4
N_DEV = 8
LANES = 128
PACK_W = 1024
VMEM_LIMIT = 56 * 1024 * 1024
MESH = pl.DeviceIdType.MESH

WEIGHT_NAMES = ('ffn1_pre_g', 'ffn1_post_g', 'ffn1_w_gate', 'ffn1_w_up', 'ffn1_w_down', 'mix_pre_g', 'mix_post_g',
                'w_in', 'mla_q_norm_g', 'mla_w_uq', 'mla_kv_norm_g', 'mla_w_ukv', 'w_o', 'ffn2_pre_g', 'ffn2_post_g',
                'ffn2_w_gate', 'ffn2_w_up', 'ffn2_w_down')
GAIN_NAMES = tuple(n for n in WEIGHT_NAMES if n.endswith('_g'))
MATRIX_NAMES = tuple(n for n in WEIGHT_NAMES if not n.endswith('_g'))


def _tile(dim, pref, mult=LANES):
    if dim <= pref:
        return dim
    t = (pref // mult) * mult
    while t >= mult:
        if dim % t == 0:
            return t
        t -= mult
    return dim


def _params(*sem):
    return pltpu.CompilerParams(dimension_semantics=sem, vmem_limit_bytes=VMEM_LIMIT)


def _dot(a, b, dims):
    return lax.dot_general(a.astype(BF16), b.astype(BF16), (dims, ((), ())), preferred_element_type=F32)


_NN = ((1,), (0,))
_NT = ((1,), (1,))
_TN = ((0,), (0,))


def _mm(pairs, mode, out_dtype, name, tm=1024, tn=512, tk=512, b_slabs=False, out_slabs=1):
    a0, b0 = pairs[0]
    if mode == 'nn':
        m, k = a0.shape
        n = b0.shape[0] * b0.shape[2] if b_slabs else b0.shape[1]
        tn = b0.shape[2] if b_slabs else tn
    elif mode == 'nt':
        m, k = a0.shape
        n = b0.shape[1] if b_slabs else b0.shape[0]
        tk = b0.shape[2] if b_slabs else tk
    else:
        (k, m), n = a0.shape, b0.shape[1]
        tn = n // out_slabs if out_slabs > 1 else tn
    tm, tn, tk = _tile(m, tm), _tile(n, tn), _tile(k, tk)
    nk = k // tk
    dims = {'nn': _NN, 'nt': _NT, 'tn': _TN}[mode]
    if mode == 'tn':
        a_spec = pl.BlockSpec((tk, tm), lambda i, j, kk: (kk, i))
    else:
        a_spec = pl.BlockSpec((tm, tk), lambda i, j, kk: (i, kk))
    if mode == 'nt':
        b_spec = (pl.BlockSpec((None, tn, tk), lambda i, j, kk: (kk, j, 0)) if b_slabs
                  else pl.BlockSpec((tn, tk), lambda i, j, kk: (j, kk)))
    else:
        b_spec = (pl.BlockSpec((None, tk, tn), lambda i, j, kk: (j, kk, 0)) if b_slabs
                  else pl.BlockSpec((tk, tn), lambda i, j, kk: (kk, j)))
    if out_slabs > 1:
        out_spec = pl.BlockSpec((None, tm, tn), lambda i, j, kk: (j, i, 0))
        out_shape = jax.ShapeDtypeStruct((out_slabs, m, tn), out_dtype)
    else:
        out_spec = pl.BlockSpec((tm, tn), lambda i, j, kk: (i, j))
        out_shape = jax.ShapeDtypeStruct((m, n), out_dtype)
    n_pairs = len(pairs)

    def body(*refs):
        o_ref = refs[2 * n_pairs]

        def partial_sum():
            s = _dot(refs[0][...], refs[1][...], dims)
            for p in range(1, n_pairs):
                s = s + _dot(refs[2 * p][...], refs[2 * p + 1][...], dims)
            return s

        if nk == 1:
            o_ref[...] = partial_sum().astype(out_dtype)
        else:
            acc = refs[2 * n_pairs + 1]
            kk = pl.program_id(2)

            @pl.when(kk == 0)
            def _():
                acc[...] = jnp.zeros_like(acc)

            acc[...] += partial_sum()

            @pl.when(kk == nk - 1)
            def _():
                o_ref[...] = acc[...].astype(out_dtype)

    return pl.pallas_call(
        body, name=name, grid=(m // tm, n // tn, nk),
        in_specs=[a_spec, b_spec] * n_pairs,
        out_specs=out_spec, out_shape=out_shape,
        scratch_shapes=[pltpu.VMEM((tm, tn), F32)] if nk > 1 else [],
        compiler_params=_params("parallel", "parallel", "arbitrary"),
    )(*[t for pair in pairs for t in pair])


def _ffn_up(h, wg, wu, name):
    m, d = h.shape
    n_slabs, _, fs = wg.shape
    tm = _tile(m, 512)

    def body(h_ref, wg_ref, wu_ref, g_ref, u_ref, a_ref):
        g = _dot(h_ref[...], wg_ref[...], _NN)
        u = _dot(h_ref[...], wu_ref[...], _NN)
        g_ref[...] = g.astype(BF16)
        u_ref[...] = u.astype(BF16)
        a_ref[...] = (g * jax.nn.sigmoid(g) * u).astype(BF16)

    w_spec = pl.BlockSpec((None, d, fs), lambda j, i: (j, 0, 0))
    o_spec = pl.BlockSpec((tm, fs), lambda j, i: (i, j))
    shape = jax.ShapeDtypeStruct((m, n_slabs * fs), BF16)
    return pl.pallas_call(
        body, name=name, grid=(n_slabs, m // tm),
        in_specs=[pl.BlockSpec((tm, d), lambda j, i: (i, 0)), w_spec, w_spec],
        out_specs=[o_spec] * 3, out_shape=[shape] * 3,
        compiler_params=_params("parallel", "arbitrary"),
    )(h, wg, wu)


def _ffn_bwd_act(dz, wd, g, u, name):
    m, d = dz.shape
    f = wd.shape[0]
    tm, tn = _tile(m, 1024), _tile(f, 512)

    def body(dz_ref, wd_ref, g_ref, u_ref, dg_ref, du_ref):
        da = _dot(dz_ref[...], wd_ref[...], _NT)
        gg = g_ref[...].astype(F32)
        uu = u_ref[...].astype(F32)
        sg = jax.nn.sigmoid(gg)
        du_ref[...] = (da * (gg * sg)).astype(BF16)
        dg_ref[...] = (da * uu * (sg * (1.0 + gg * (1.0 - sg)))).astype(BF16)

    t_spec = pl.BlockSpec((tm, tn), lambda i, j: (i, j))
    shape = jax.ShapeDtypeStruct((m, f), BF16)
    return pl.pallas_call(
        body, name=name, grid=(m // tm, f // tn),
        in_specs=[pl.BlockSpec((tm, d), lambda i, j: (i, 0)), pl.BlockSpec((tn, d), lambda i, j: (j, 0)), t_spec, t_spec],
        out_specs=[t_spec] * 2, out_shape=[shape] * 2,
        compiler_params=_params("parallel", "arbitrary"),
    )(dz, wd, g, u)


def _rms(x, g):
    r = lax.rsqrt(jnp.mean(x * x, axis=-1, keepdims=True) + RMS_EPS)
    return x * r * g


def _rms_fwd(x, g, name, col_block=0, out_dtype=BF16):
    t = x.shape[0]
    w = g.shape[-1]
    tr = _tile(t, 256)

    def body(x_ref, g_ref, o_ref):
        o_ref[...] = _rms(x_ref[...].astype(F32), g_ref[...]).astype(out_dtype)

    return pl.pallas_call(
        body, name=name, grid=(t // tr,),
        in_specs=[pl.BlockSpec((tr, w), lambda i: (i, col_block)), pl.BlockSpec((1, w), lambda i: (0, 0))],
        out_specs=pl.BlockSpec((tr, w), lambda i: (i, 0)),
        out_shape=jax.ShapeDtypeStruct((t, w), out_dtype),
        compiler_params=_params("parallel"),
    )(x, g.reshape(1, w))


def _post_pre(x, z, g_post, alpha, g_next, name):
    t, w = x.shape
    tr = _tile(t, 256)

    def body(x_ref, z_ref, gp_ref, gn_ref, xo_ref, h_ref):
        xn = x_ref[...] + alpha * _rms(z_ref[...], gp_ref[...])
        xo_ref[...] = xn
        h_ref[...] = _rms(xn, gn_ref[...]).astype(BF16)

    row = pl.BlockSpec((tr, w), lambda i: (i, 0))
    vec = pl.BlockSpec((1, w), lambda i: (0, 0))
    return pl.pallas_call(
        body, name=name, grid=(t // tr,),
        in_specs=[row, row, vec, vec], out_specs=[row, row],
        out_shape=[jax.ShapeDtypeStruct((t, w), F32), jax.ShapeDtypeStruct((t, w), BF16)],
        compiler_params=_params("parallel"),
    )(x, z, g_post.reshape(1, w), g_next.reshape(1, w))


def _post_loss(x, z, g_post, alpha, target, name):
    t, w = x.shape
    tr = _tile(t, 256)

    def body(x_ref, z_ref, gp_ref, t_ref, dy_ref, loss_ref):
        err = x_ref[...] + alpha * _rms(z_ref[...], gp_ref[...]) - t_ref[...]
        dy_ref[...] = err * (1.0 / w)

        @pl.when(pl.program_id(0) == 0)
        def _():
            loss_ref[...] = jnp.zeros_like(loss_ref)

        loss_ref[...] += jnp.sum(jnp.mean(err * err, axis=-1, keepdims=True), axis=0, keepdims=True) * 0.5

    row = pl.BlockSpec((tr, w), lambda i: (i, 0))
    vec = pl.BlockSpec((1, w), lambda i: (0, 0))
    return pl.pallas_call(
        body, name=name, grid=(t // tr,),
        in_specs=[row, row, vec, row], out_specs=[row, pl.BlockSpec((1, 1), lambda i: (0, 0))],
        out_shape=[jax.ShapeDtypeStruct((t, w), F32), jax.ShapeDtypeStruct((1, 1), F32)],
        compiler_params=_params("arbitrary"),
    )(x, z, g_post.reshape(1, w), target)


def _rms_bwd(x, g, dy, alpha, name, res=None, col_block=0, out_dtype=F32):
    t = x.shape[0]
    w = g.shape[-1]
    tr = _tile(t, 256)
    has_res = res is not None

    def body(*refs):
        x_ref, g_ref, dy_ref = refs[:3]
        dx_ref, dg_ref = refs[-2:]
        xx = x_ref[...].astype(F32)
        dyy = dy_ref[...].astype(F32) * alpha
        r = lax.rsqrt(jnp.mean(xx * xx, axis=-1, keepdims=True) + RMS_EPS)
        xh = xx * r
        gy = dyy * g_ref[...]
        dx = r * (gy - xh * jnp.mean(gy * xh, axis=-1, keepdims=True))
        if has_res:
            dx = dx + refs[3][...]
        dx_ref[...] = dx.astype(out_dtype)

        @pl.when(pl.program_id(0) == 0)
        def _():
            dg_ref[...] = jnp.zeros_like(dg_ref)

        dg_ref[...] += jnp.sum(dyy * xh, axis=0, keepdims=True)

    row = pl.BlockSpec((tr, w), lambda i: (i, 0))
    vec = pl.BlockSpec((1, w), lambda i: (0, 0))
    ins = [x, g.reshape(1, w), dy] + ([res] if has_res else [])
    dx, dg = pl.pallas_call(
        body, name=name, grid=(t // tr,),
        in_specs=[pl.BlockSpec((tr, w), lambda i: (i, col_block)), vec, row] + ([row] if has_res else []),
        out_specs=[row, vec],
        out_shape=[jax.ShapeDtypeStruct((t, w), out_dtype), jax.ShapeDtypeStruct((1, w), F32)],
        compiler_params=_params("arbitrary"),
    )(*ins)
    return dx, dg.reshape(w)


def _rot_tables(cos, sin, period, start, fwd):
    t, s = cos.shape
    one = jnp.ones((t, start), F32)
    tail_w = period - start - 2 * s
    tail = jnp.ones((t, tail_w), F32) if start == 0 else jnp.zeros((t, tail_w), F32)
    z = lambda w: jnp.zeros((t, w), F32)
    c = jnp.concatenate([one, cos, cos, tail], axis=1)
    sm = jnp.concatenate([z(start + s), sin, z(tail_w)], axis=1)
    sp = jnp.concatenate([z(start), -sin, z(s + tail_w)], axis=1)
    if fwd:
        return c, sm, sp
    return c, jnp.roll(sp, s, axis=1), jnp.roll(sm, -s, axis=1)


def _rot_apply(x, c, sm, sp, shift):
    w = x.shape[-1]
    return x * c + pltpu.roll(x, shift, 1) * sm + pltpu.roll(x, w - shift, 1) * sp


def _rot(xs, tables, shift, name, col_block=0, width=None, out_dtype=BF16):
    t = xs[0].shape[0]
    w = width if width is not None else xs[0].shape[1]
    tr = _tile(t, 256)
    n_in = len(xs)
    period = tables[0].shape[1] if tables is not None else w

    def body(*refs):
        o_ref = refs[-1]
        for hh in range(w // period):
            sl = slice(hh * period, (hh + 1) * period)
            v = refs[0][:, sl].astype(F32)
            for p in range(1, n_in):
                v = v + refs[p][:, sl].astype(F32)
            if tables is not None:
                c_ref, sm_ref, sp_ref = refs[n_in:n_in + 3]
                v = _rot_apply(v, c_ref[...], sm_ref[...], sp_ref[...], shift)
            o_ref[:, sl] = v.astype(out_dtype)

    tab = pl.BlockSpec((tr, period), lambda i: (i, 0))
    return pl.pallas_call(
        body, name=name, grid=(t // tr,),
        in_specs=[pl.BlockSpec((tr, w), lambda i: (i, col_block))] * n_in + ([tab] * 3 if tables is not None else []),
        out_specs=pl.BlockSpec((tr, w), lambda i: (i, 0)),
        out_shape=jax.ShapeDtypeStruct((t, w), out_dtype),
        compiler_params=_params("parallel"),
    )(*xs, *(tables if tables is not None else ()))


def _mla_kfull(kv, proj, kr_block, tables, heads, name):
    t = kv.shape[0]
    tr = _tile(t, 256)

    def body(kv_ref, kr_ref, c_ref, sm_ref, sp_ref, o_ref):
        kr = _rot_apply(kr_ref[...].astype(F32), c_ref[...], sm_ref[...], sp_ref[...], MLA_ROPE // 2).astype(BF16)
        for hh in range(heads):
            o_ref[:, hh * 256:hh * 256 + 128] = kv_ref[:, hh * 256:hh * 256 + 128]
            o_ref[:, hh * 256 + 128:(hh + 1) * 256] = kr

    tab = pl.BlockSpec((tr, LANES), lambda i: (i, 0))
    full = pl.BlockSpec((tr, heads * 256), lambda i: (i, 0))
    return pl.pallas_call(
        body, name=name, grid=(t // tr,),
        in_specs=[full, pl.BlockSpec((tr, LANES), lambda i: (i, kr_block)), tab, tab, tab],
        out_specs=full, out_shape=jax.ShapeDtypeStruct((t, heads * 256), BF16),
        compiler_params=_params("parallel"),
    )(kv, proj, *tables)


def _mla_dkv(dk, dv, tables, heads, name):
    t = dk.shape[0]
    tr = _tile(t, 256)

    def body(dk_ref, dv_ref, c_ref, sm_ref, sp_ref, dkv_ref, dkr_ref):
        acc = jnp.zeros((tr, LANES), F32)
        for hh in range(heads):
            dkv_ref[:, hh * 256:hh * 256 + 128] = dk_ref[:, hh * 256:hh * 256 + 128].astype(BF16)
            dkv_ref[:, hh * 256 + 128:(hh + 1) * 256] = dv_ref[:, hh * 128:(hh + 1) * 128].astype(BF16)
            acc = acc + dk_ref[:, hh * 256 + 128:(hh + 1) * 256]
        dkr_ref[...] = _rot_apply(acc, c_ref[...], sm_ref[...], sp_ref[...], MLA_ROPE // 2).astype(BF16)

    tab = pl.BlockSpec((tr, LANES), lambda i: (i, 0))
    full = pl.BlockSpec((tr, heads * 256), lambda i: (i, 0))
    return pl.pallas_call(
        body, name=name, grid=(t // tr,),
        in_specs=[full, pl.BlockSpec((tr, heads * 128), lambda i: (i, 0)), tab, tab, tab],
        out_specs=[full, tab],
        out_shape=[jax.ShapeDtypeStruct((t, heads * 256), BF16), jax.ShapeDtypeStruct((t, LANES), BF16)],
        compiler_params=_params("parallel"),
    )(dk, dv, *tables)


def _delta(do, o, name):
    t, w = do.shape
    tr = _tile(t, 256)

    def body(do_ref, o_ref, d_ref):
        for hh in range(w // HEAD_DIM):
            sl = slice(hh * HEAD_DIM, (hh + 1) * HEAD_DIM)
            s = jnp.sum(do_ref[:, sl].astype(F32) * o_ref[:, sl].astype(F32), axis=1, keepdims=True)
            d_ref[:, sl] = jnp.broadcast_to(s, (tr, HEAD_DIM))

    row = pl.BlockSpec((tr, w), lambda i: (i, 0))
    return pl.pallas_call(
        body, name=name, grid=(t // tr,), in_specs=[row, row], out_specs=row,
        out_shape=jax.ShapeDtypeStruct((t, w), F32), compiler_params=_params("parallel"),
    )(do, o)


def _causal_mask(s, row0, col0):
    qi = row0 + lax.broadcasted_iota(jnp.int32, s.shape, 0)
    kj = col0 + lax.broadcasted_iota(jnp.int32, s.shape, 1)
    return jnp.where(kj <= qi, s, NEG)


def _head_group(heads):
    return 2 if heads % 2 == 0 else 1


def _v_specs(rows, hp, row_block):
    return [pl.BlockSpec((rows, MLA_V), functools.partial(lambda h, i, hh: (row_block(i), 2 * (h * hp + hh) + 1), hh=hh))
            for hh in range(hp)]


def _flash_fwd(q, k, kv, heads, name):
    t = q.shape[0]
    tq = _tile(t, 512)
    hp = _head_group(heads)
    nrep = tq // HEAD_DIM
    scale = 1.0 / math.sqrt(MLA_QK)

    def body(q_ref, k_ref, *rest):
        v_refs = rest[:hp]
        o_ref, lse_ref, m_sc, acc_sc = rest[hp:]
        i = pl.program_id(1)
        m_sc[...] = jnp.full_like(m_sc, NEG)
        acc_sc[...] = jnp.zeros_like(acc_sc)
        ones = jnp.ones((tq, HEAD_DIM), BF16)

        def step(j, masked):
            rows = pl.ds(pl.multiple_of(j * tq, tq), tq)
            for hh in range(hp):
                hs = slice(hh * 256, (hh + 1) * 256)
                s = _dot(q_ref[:, hs], k_ref[rows, hs], _NT) * scale
                if masked:
                    s = _causal_mask(s, 0, 0)
                m_prev = m_sc[hh]
                m_new = jnp.maximum(m_prev, jnp.broadcast_to(jnp.max(s, axis=1, keepdims=True), (tq, HEAD_DIM)))
                a = jnp.exp(m_prev - m_new)
                p = jnp.exp(s - jnp.tile(m_new, (1, nrep)))
                pv = _dot(p, jnp.concatenate([v_refs[hh][rows, :], ones], axis=1), _NN)
                acc_sc[hh] = jnp.tile(a, (1, 2)) * acc_sc[hh] + pv
                m_sc[hh] = m_new

        def loop_body(j, carry):
            step(j, False)
            return carry

        lax.fori_loop(0, i, loop_body, 0)
        step(i, True)
        for hh in range(hp):
            l = acc_sc[hh][:, HEAD_DIM:]
            o_ref[:, hh * MLA_V:(hh + 1) * MLA_V] = (acc_sc[hh][:, :HEAD_DIM] / l).astype(BF16)
            lse_ref[:, hh * HEAD_DIM:(hh + 1) * HEAD_DIM] = m_sc[hh] + jnp.log(l)

    stat = pl.BlockSpec((tq, hp * HEAD_DIM), lambda h, i: (i, h))
    return pl.pallas_call(
        body, name=name, grid=(heads // hp, t // tq),
        in_specs=[pl.BlockSpec((tq, hp * 256), lambda h, i: (i, h)), pl.BlockSpec((t, hp * 256), lambda h, i: (0, h))]
        + _v_specs(t, hp, lambda i: 0),
        out_specs=[stat, stat],
        out_shape=[jax.ShapeDtypeStruct((t, heads * MLA_V), BF16), jax.ShapeDtypeStruct((t, heads * HEAD_DIM), F32)],
        scratch_shapes=[pltpu.VMEM((hp, tq, HEAD_DIM), F32), pltpu.VMEM((hp, tq, 2 * HEAD_DIM), F32)],
        compiler_params=_params("parallel", "arbitrary"),
    )(q, k, *([kv] * hp))


def _flash_bwd_dq(q, k, kv, do, lse, delta, heads, name):
    t = q.shape[0]
    tq = _tile(t, 512)
    hp = _head_group(heads)
    nrep = tq // HEAD_DIM
    scale = 1.0 / math.sqrt(MLA_QK)

    def body(q_ref, k_ref, *rest):
        v_refs = rest[:hp]
        do_ref, lse_ref, d_ref, dq_ref = rest[hp:]
        i = pl.program_id(1)
        dq_ref[...] = jnp.zeros_like(dq_ref)

        def step(j, masked):
            rows = pl.ds(pl.multiple_of(j * tq, tq), tq)
            for hh in range(hp):
                hs = slice(hh * 256, (hh + 1) * 256)
                st = slice(hh * HEAD_DIM, (hh + 1) * HEAD_DIM)
                kk = k_ref[rows, hs]
                s = _dot(q_ref[:, hs], kk, _NT) * scale
                if masked:
                    s = _causal_mask(s, 0, 0)
                p = jnp.exp(s - jnp.tile(lse_ref[:, st], (1, nrep)))
                dp = _dot(do_ref[:, st], v_refs[hh][rows, :], _NT)
                ds = p * (dp - jnp.tile(d_ref[:, st], (1, nrep))) * scale
                dq_ref[:, hs] += _dot(ds, kk, _NN)

        def loop_body(j, carry):
            step(j, False)
            return carry

        lax.fori_loop(0, i, loop_body, 0)
        step(i, True)

    stat = pl.BlockSpec((tq, hp * HEAD_DIM), lambda h, i: (i, h))
    qs = pl.BlockSpec((tq, hp * 256), lambda h, i: (i, h))
    return pl.pallas_call(
        body, name=name, grid=(heads // hp, t // tq),
        in_specs=[qs, pl.BlockSpec((t, hp * 256), lambda h, i: (0, h))] + _v_specs(t, hp, lambda i: 0) + [stat, stat, stat],
        out_specs=qs, out_shape=jax.ShapeDtypeStruct((t, heads * 256), F32),
        compiler_params=_params("parallel", "arbitrary"),
    )(q, k, *([kv] * hp), do, lse, delta)


def _stat_rows(stat, heads):
    return jnp.transpose(stat[:, ::HEAD_DIM]).reshape(heads, 1, stat.shape[0])


def _flash_bwd_dkv(q, k, kv, do, lse_t, delta_t, heads, name):
    t = q.shape[0]
    tq = _tile(t, 512)
    nq = t // tq
    hp = _head_group(heads)
    scale = 1.0 / math.sqrt(MLA_QK)

    def body(q_ref, k_ref, *rest):
        v_refs = rest[:hp]
        do_ref, lse_ref, d_ref, dk_ref, dv_ref = rest[hp:]
        j = pl.program_id(1)
        dk_ref[...] = jnp.zeros_like(dk_ref)
        dv_ref[...] = jnp.zeros_like(dv_ref)

        def step(i, masked):
            rows = pl.ds(pl.multiple_of(i * tq, tq), tq)
            for hh in range(hp):
                hs = slice(hh * 256, (hh + 1) * 256)
                st = slice(hh * HEAD_DIM, (hh + 1) * HEAD_DIM)
                qq = q_ref[rows, hs]
                dd = do_ref[rows, st]
                s = _dot(k_ref[:, hs], qq, _NT) * scale
                if masked:
                    kj = lax.broadcasted_iota(jnp.int32, s.shape, 0)
                    qi = lax.broadcasted_iota(jnp.int32, s.shape, 1)
                    s = jnp.where(kj <= qi, s, NEG)
                p = jnp.exp(s - lse_ref[hh, :, rows])
                dv_ref[:, st] += _dot(p, dd, _NN)
                dp = _dot(v_refs[hh][...], dd, _NT)
                ds = p * (dp - d_ref[hh, :, rows]) * scale
                dk_ref[:, hs] += _dot(ds, qq, _NN)

        def loop_body(i, carry):
            step(i, False)
            return carry

        step(j, True)
        lax.fori_loop(j + 1, nq, loop_body, 0)

    ks = pl.BlockSpec((tq, hp * 256), lambda h, j: (j, h))
    row = pl.BlockSpec((hp, 1, t), lambda h, j: (h, 0, 0))
    return pl.pallas_call(
        body, name=name, grid=(heads // hp, nq),
        in_specs=[pl.BlockSpec((t, hp * 256), lambda h, j: (0, h)), ks] + _v_specs(tq, hp, lambda j: j)
        + [pl.BlockSpec((t, hp * MLA_V), lambda h, j: (0, h)), row, row],
        out_specs=[ks, pl.BlockSpec((tq, hp * MLA_V), lambda h, j: (j, h))],
        out_shape=[jax.ShapeDtypeStruct((t, heads * 256), F32), jax.ShapeDtypeStruct((t, heads * MLA_V), F32)],
        compiler_params=_params("parallel", "arbitrary"),
    )(q, k, *([kv] * hp), do, lse_t, delta_t)


def _dil_view(a, d):
    t, w = a.shape
    return a.reshape(t // d, d * w)


def _dil_masks(first_block):
    qi = lax.broadcasted_iota(jnp.int32, (DIL_BLK, DIL_BLK), 0)
    kj = lax.broadcasted_iota(jnp.int32, (DIL_BLK, DIL_BLK), 1)
    return jnp.logical_and(kj >= qi, jnp.logical_not(first_block)), kj <= qi


def _dil_fwd(q, k, v, d, name):
    t, w = q.shape
    nb = t // d // DIL_BLK
    scale = 1.0 / math.sqrt(HEAD_DIM)

    def body(q_ref, kp_ref, kc_ref, vp_ref, vc_ref, o_ref, lse_ref):
        mask_p, mask_c = _dil_masks(pl.program_id(1) == 0)
        for hh in range(w // HEAD_DIM):
            sl = slice(hh * HEAD_DIM, (hh + 1) * HEAD_DIM)
            qh = q_ref[:, sl]
            sp = jnp.where(mask_p, _dot(qh, kp_ref[:, sl], _NT) * scale, NEG)
            sc = jnp.where(mask_c, _dot(qh, kc_ref[:, sl], _NT) * scale, NEG)
            m = jnp.maximum(jnp.max(sp, axis=1, keepdims=True), jnp.max(sc, axis=1, keepdims=True))
            pp = jnp.exp(sp - m)
            pc = jnp.exp(sc - m)
            l = jnp.sum(pp, axis=1, keepdims=True) + jnp.sum(pc, axis=1, keepdims=True)
            o_ref[:, sl] = (_dot(pp, vp_ref[:, sl], _NN) + _dot(pc, vc_ref[:, sl], _NN)) / l
            lse_ref[:, sl] = jnp.broadcast_to(m + jnp.log(l), (DIL_BLK, HEAD_DIM))

    cur = pl.BlockSpec((DIL_BLK, w), lambda r, n: (n, r))
    prev = pl.BlockSpec((DIL_BLK, w), lambda r, n: (jnp.maximum(n - 1, 0), r))
    shape = jax.ShapeDtypeStruct((t // d, d * w), F32)
    o, lse = pl.pallas_call(
        body, name=name, grid=(d, nb),
        in_specs=[cur, prev, cur, prev, cur], out_specs=[cur, cur], out_shape=[shape, shape],
        compiler_params=_params("parallel", "arbitrary"),
    )(_dil_view(q, d), _dil_view(k, d), _dil_view(k, d), _dil_view(v, d), _dil_view(v, d))
    return o.reshape(t, w), lse.reshape(t, w)


def _dil_combine(os_, lses, name):
    t, w = os_[0].shape
    tr = _tile(t, 256)
    n = len(os_)

    def body(*refs):
        o_ref, m_ref = refs[-2:]
        ls = [refs[n + p][...] for p in range(n)]
        m = functools.reduce(jnp.maximum, ls)
        es = [jnp.exp(l - m) for l in ls]
        den = functools.reduce(jnp.add, es)
        acc = es[0] / den * refs[0][...]
        for p in range(1, n):
            acc = acc + es[p] / den * refs[p][...]
        o_ref[...] = acc.astype(BF16)
        m_ref[...] = m + jnp.log(den)

    row = pl.BlockSpec((tr, w), lambda i: (i, 0))
    return pl.pallas_call(
        body, name=name, grid=(t // tr,), in_specs=[row] * (2 * n), out_specs=[row, row],
        out_shape=[jax.ShapeDtypeStruct((t, w), BF16), jax.ShapeDtypeStruct((t, w), F32)],
        compiler_params=_params("parallel"),
    )(*os_, *lses)


def _dil_bwd_dq(q, k, v, do, mt, delta, d, name):
    t, w = q.shape
    nb = t // d // DIL_BLK
    scale = 1.0 / math.sqrt(HEAD_DIM)

    def body(q_ref, kp_ref, kc_ref, vp_ref, vc_ref, do_ref, mt_ref, d_ref, dq_ref):
        mask_p, mask_c = _dil_masks(pl.program_id(1) == 0)
        for hh in range(w // HEAD_DIM):
            sl = slice(hh * HEAD_DIM, (hh + 1) * HEAD_DIM)
            qh = q_ref[:, sl]
            dd = do_ref[:, sl]
            mrow = mt_ref[:, hh * HEAD_DIM:hh * HEAD_DIM + 1]
            drow = d_ref[:, hh * HEAD_DIM:hh * HEAD_DIM + 1]
            sp = jnp.where(mask_p, _dot(qh, kp_ref[:, sl], _NT) * scale, NEG)
            sc = jnp.where(mask_c, _dot(qh, kc_ref[:, sl], _NT) * scale, NEG)
            dsp = jnp.exp(sp - mrow) * (_dot(dd, vp_ref[:, sl], _NT) - drow) * scale
            dsc = jnp.exp(sc - mrow) * (_dot(dd, vc_ref[:, sl], _NT) - drow) * scale
            dq_ref[:, sl] = _dot(dsp, kp_ref[:, sl], _NN) + _dot(dsc, kc_ref[:, sl], _NN)

    cur = pl.BlockSpec((DIL_BLK, w), lambda r, n: (n, r))
    prev = pl.BlockSpec((DIL_BLK, w), lambda r, n: (jnp.maximum(n - 1, 0), r))
    kv_, vv_ = _dil_view(k, d), _dil_view(v, d)
    dq = pl.pallas_call(
        body, name=name, grid=(d, nb),
        in_specs=[cur, prev, cur, prev, cur, cur, cur, cur], out_specs=cur,
        out_shape=jax.ShapeDtypeStruct((t // d, d * w), F32),
        compiler_params=_params("parallel", "arbitrary"),
    )(_dil_view(q, d), kv_, kv_, vv_, vv_, _dil_view(do, d), _dil_view(mt, d), _dil_view(delta, d))
    return dq.reshape(t, w)


def _dil_bwd_dkv(q, k, v, do, mt, delta, d, name):
    t, w = q.shape
    nb = t // d // DIL_BLK
    scale = 1.0 / math.sqrt(HEAD_DIM)

    def body(k_ref, v_ref, qs_ref, qn_ref, dos_ref, don_ref, ms_ref, mn_ref, ds_ref, dn_ref, dk_ref, dv_ref):
        last = pl.program_id(1) == nb - 1
        qi = lax.broadcasted_iota(jnp.int32, (DIL_BLK, DIL_BLK), 0)
        kj = lax.broadcasted_iota(jnp.int32, (DIL_BLK, DIL_BLK), 1)
        mask_s = kj <= qi
        mask_n = jnp.logical_and(kj >= qi, jnp.logical_not(last))
        for hh in range(w // HEAD_DIM):
            sl = slice(hh * HEAD_DIM, (hh + 1) * HEAD_DIM)
            st = slice(hh * HEAD_DIM, hh * HEAD_DIM + 1)
            kh, vh = k_ref[:, sl], v_ref[:, sl]
            dk = jnp.zeros((DIL_BLK, HEAD_DIM), F32)
            dv = jnp.zeros((DIL_BLK, HEAD_DIM), F32)
            for q_ref, do_ref, m_ref, d_ref, mask in ((qs_ref, dos_ref, ms_ref, ds_ref, mask_s),
                                                      (qn_ref, don_ref, mn_ref, dn_ref, mask_n)):
                qh, dd = q_ref[:, sl], do_ref[:, sl]
                s = jnp.where(mask, _dot(qh, kh, _NT) * scale, NEG)
                p = jnp.exp(s - m_ref[:, st])
                dv = dv + _dot(p, dd, _TN)
                dsv = p * (_dot(dd, vh, _NT) - d_ref[:, st]) * scale
                dk = dk + _dot(dsv, qh, _TN)
            dk_ref[:, sl] = dk
            dv_ref[:, sl] = dv

    cur = pl.BlockSpec((DIL_BLK, w), lambda r, n: (n, r))
    nxt = pl.BlockSpec((DIL_BLK, w), lambda r, n: (jnp.minimum(n + 1, nb - 1), r))
    shape = jax.ShapeDtypeStruct((t // d, d * w), F32)
    qv, dov, mv, dv_ = _dil_view(q, d), _dil_view(do, d), _dil_view(mt, d), _dil_view(delta, d)
    dk, dv = pl.pallas_call(
        body, name=name, grid=(d, nb),
        in_specs=[cur, cur, cur, nxt, cur, nxt, cur, nxt, cur, nxt], out_specs=[cur, cur], out_shape=[shape, shape],
        compiler_params=_params("parallel", "arbitrary"),
    )(_dil_view(k, d), _dil_view(v, d), qv, qv, dov, dov, mv, mv, dv_, dv_)
    return dk.reshape(t, w), dv.reshape(t, w)


def _adamw(w, g, m, v, name):
    r, c = w.shape
    tr = _tile(r, 256, 8)
    c1 = 1.0 - ADAM_B1 ** ADAM_STEP
    c2 = 1.0 - ADAM_B2 ** ADAM_STEP

    def body(w_ref, g_ref, m_ref, v_ref, d_ref, mo_ref, vo_ref):
        gg = g_ref[...]
        mn = ADAM_B1 * m_ref[...] + (1.0 - ADAM_B1) * gg
        vn = ADAM_B2 * v_ref[...] + (1.0 - ADAM_B2) * (gg * gg)
        mo_ref[...] = mn
        vo_ref[...] = vn
        d_ref[...] = -ADAM_LR * ((mn / c1) / (jnp.sqrt(vn / c2) + ADAM_EPS) + ADAM_WD * w_ref[...])

    row = pl.BlockSpec((tr, c), lambda i: (i, 0))
    shape = jax.ShapeDtypeStruct((r, c), F32)
    return pl.pallas_call(
        body, name=name, grid=(r // tr,), in_specs=[row] * 4, out_specs=[row] * 3, out_shape=[shape] * 3,
        compiler_params=_params("parallel"),
    )(w, g, m, v)


def _position():
    return lax.axis_index("x"), lax.axis_index("y"), lax.axis_index("c")


def _other_chips(x, y):
    return [(1 - x, y), (x, 1 - y), (1 - x, 1 - y)]


_ANY = pl.BlockSpec(memory_space=pl.ANY)


def _half_rows(a, half):
    return pl.ds(pl.multiple_of(half * (a // 2), 16), a // 2)


def _allgather_shards(shards, name):
    n = len(shards)

    def body(*refs):
        w_refs, out_refs = refs[:n], refs[n:2 * n]
        send_sems, recv_sems, local_sems = refs[2 * n:]
        x, y, c = _position()
        me = 2 * x + y
        sibling = (x, y, 1 - c)
        chips = _other_chips(x, y)

        def copy(level, kk, i, src, dst, to):
            return pltpu.make_async_remote_copy(src_ref=src, dst_ref=dst, send_sem=send_sems.at[level, kk, i],
                                                recv_sem=recv_sems.at[level, kk, i], device_id=to, device_id_type=MESH)

        mine = [pltpu.make_async_copy(w_refs[i], out_refs[i].at[me], local_sems.at[i]) for i in range(n)]
        for cp in mine:
            cp.start()
        started = []
        for i in range(n):
            rows = _half_rows(shards[i].shape[0], c)
            for kk, (px, py) in enumerate(chips):
                cp = copy(0, kk, i, w_refs[i].at[rows], out_refs[i].at[me, rows], (px, py, c))
                cp.start()
                started.append(cp)
        for i in range(n):
            rows = _half_rows(shards[i].shape[0], c)
            for kk, (px, py) in enumerate(chips):
                landed = out_refs[i].at[2 * px + py, rows]
                copy(0, kk, i, landed, landed, (px, py, c)).wait_recv()
                cp = copy(1, kk, i, landed, landed, sibling)
                cp.start()
                started.append(cp)
        for i in range(n):
            rows = _half_rows(shards[i].shape[0], 1 - c)
            for kk, (px, py) in enumerate(chips):
                passed = out_refs[i].at[2 * px + py, rows]
                copy(1, kk, i, passed, passed, sibling).wait_recv()
        for cp in started:
            cp.wait_send()
        for cp in mine:
            cp.wait()

    return pl.pallas_call(
        body, name=name, in_specs=[_ANY] * n, out_specs=[_ANY] * n,
        out_shape=[jax.ShapeDtypeStruct((N_CHIPS,) + s.shape, s.dtype) for s in shards],
        scratch_shapes=[pltpu.SemaphoreType.DMA((2, 3, n)), pltpu.SemaphoreType.DMA((2, 3, n)),
                        pltpu.SemaphoreType.DMA((n,))],
    )(*shards)


def _swap_halves(gs, name):
    n = len(gs)

    def body(*refs):
        g_refs, out_refs = refs[:n], refs[n:2 * n]
        send_sems, recv_sems = refs[2 * n:]
        x, y, c = _position()
        cps = []
        for i in range(n):
            cp = pltpu.make_async_remote_copy(
                src_ref=g_refs[i].at[:, _half_rows(gs[i].shape[1], 1 - c)], dst_ref=out_refs[i],
                send_sem=send_sems.at[i], recv_sem=recv_sems.at[i], device_id=(x, y, 1 - c), device_id_type=MESH)
            cp.start()
            cps.append(cp)
        for cp in cps:
            cp.wait_recv()
        for cp in cps:
            cp.wait_send()

    return pl.pallas_call(
        body, name=name, in_specs=[_ANY] * n, out_specs=[_ANY] * n,
        out_shape=[jax.ShapeDtypeStruct((g.shape[0], g.shape[1] // 2, g.shape[2]), g.dtype) for g in gs],
        scratch_shapes=[pltpu.SemaphoreType.DMA((n,)), pltpu.SemaphoreType.DMA((n,))],
    )(*gs)


def _exchange_chips(sums, name):
    n = len(sums)

    def body(*refs):
        s_refs, out_refs = refs[:n], refs[n:2 * n]
        send_sems, recv_sems = refs[2 * n:]
        x, y, c = _position()
        cps = []
        for i in range(n):
            for kk, (px, py) in enumerate(_other_chips(x, y)):
                cp = pltpu.make_async_remote_copy(
                    src_ref=s_refs[i].at[2 * px + py], dst_ref=out_refs[i].at[kk], send_sem=send_sems.at[i, kk],
                    recv_sem=recv_sems.at[i, kk], device_id=(px, py, c), device_id_type=MESH)
                cp.start()
                cps.append(cp)
        for cp in cps:
            cp.wait_recv()
        for cp in cps:
            cp.wait_send()

    return pl.pallas_call(
        body, name=name, in_specs=[_ANY] * n, out_specs=[_ANY] * n,
        out_shape=[jax.ShapeDtypeStruct((3,) + s.shape[1:], s.dtype) for s in sums],
        scratch_shapes=[pltpu.SemaphoreType.DMA((n, 3)), pltpu.SemaphoreType.DMA((n, 3))],
    )(*sums)


def _share_halves(bufs, name):
    n = len(bufs)

    def body(*refs):
        in_refs, out_refs = refs[:n], refs[n:2 * n]
        send_sems, recv_sems = refs[2 * n:]
        x, y, c = _position()
        cps = []
        for i in range(n):
            cp = pltpu.make_async_remote_copy(
                src_ref=in_refs[i].at[c], dst_ref=out_refs[i].at[c], send_sem=send_sems.at[i], recv_sem=recv_sems.at[i],
                device_id=(x, y, 1 - c), device_id_type=MESH)
            cp.start()
            cps.append(cp)
        for i in range(n):
            pltpu.make_async_remote_copy(
                src_ref=in_refs[i].at[1 - c], dst_ref=out_refs[i].at[1 - c], send_sem=send_sems.at[i],
                recv_sem=recv_sems.at[i], device_id=(x, y, 1 - c), device_id_type=MESH).wait_recv()
        for cp in cps:
            cp.wait_send()

    return pl.pallas_call(
        body, name=name, in_specs=[_ANY] * n, out_specs=[_ANY] * n,
        out_shape=[jax.ShapeDtypeStruct(b.shape, b.dtype) for b in bufs],
        input_output_aliases={i: i for i in range(n)},
        scratch_shapes=[pltpu.SemaphoreType.DMA((n,)), pltpu.SemaphoreType.DMA((n,))],
    )(*bufs)


def _add_pair(g, got, name):
    n, a, b = g.shape
    ah = a // 2
    tr = _tile(ah, 256, 16)

    def body(c_ref, a_ref, b_ref, o_ref):
        o_ref[...] = (a_ref[...].astype(F32) + b_ref[...].astype(F32)).astype(BF16)

    blk = pl.BlockSpec((None, tr, b), lambda j, i, c_ref: (j, i, 0))
    return pl.pallas_call(
        body, name=name,
        grid_spec=pltpu.PrefetchScalarGridSpec(
            num_scalar_prefetch=1, grid=(n, ah // tr),
            in_specs=[pl.BlockSpec((None, None, tr, b), lambda j, i, c_ref: (j, c_ref[0], i, 0)), blk], out_specs=blk),
        out_shape=jax.ShapeDtypeStruct((n, ah, b), BF16),
        compiler_params=_params("parallel", "parallel"),
    )(lax.axis_index("c").reshape(1).astype(jnp.int32), g.reshape(n, 2, ah, b), got)


def _add_chips(sums, got, name):
    n, ah, b = sums.shape
    tr = _tile(ah, 256, 16)

    def body(at_ref, a_ref, b_ref, o_ref):
        o_ref[...] = ((a_ref[...].astype(F32) + b_ref[0].astype(F32)) + b_ref[1].astype(F32)) + b_ref[2].astype(F32)

    at = jnp.stack([2 * lax.axis_index("x") + lax.axis_index("y"), lax.axis_index("c")]).astype(jnp.int32)
    return pl.pallas_call(
        body, name=name,
        grid_spec=pltpu.PrefetchScalarGridSpec(
            num_scalar_prefetch=1, grid=(ah // tr,),
            in_specs=[pl.BlockSpec((None, tr, b), lambda i, at_ref: (at_ref[0], i, 0)),
                      pl.BlockSpec((3, tr, b), lambda i, at_ref: (0, i, 0))],
            out_specs=pl.BlockSpec((None, tr, b), lambda i, at_ref: (at_ref[1], i, 0))),
        out_shape=jax.ShapeDtypeStruct((2, ah, b), F32),
        compiler_params=_params("parallel"),
    )(at, sums, got)


def _allgather_small(v, name):
    m_per, w = v.shape

    def body(x_ref, out_ref, send_sems, recv_sems, local_sem):
        x, y, c = _position()
        me, sibling = (x, y, c), (x, y, 1 - c)
        chips = _other_chips(x, y)

        def rows(px, py, pc):
            return out_ref.at[pl.ds(pl.multiple_of((4 * px + 2 * py + pc) * m_per, 8), m_per), :]

        def copy(kk, block, to, src=None):
            return pltpu.make_async_remote_copy(
                src_ref=rows(*block) if src is None else src, dst_ref=rows(*block), send_sem=send_sems.at[kk],
                recv_sem=recv_sems.at[kk], device_id=to, device_id_type=MESH)

        mine = pltpu.make_async_copy(x_ref, rows(*me), local_sem)
        mine.start()
        first = [copy(0, me, sibling, src=x_ref)]
        first += [copy(1 + j, me, (*chip, c), src=x_ref) for j, chip in enumerate(chips)]
        for cp in first:
            cp.start()
        passed = [copy(4 + j, (*chip, c), sibling) for j, chip in enumerate(chips)]
        for j, chip in enumerate(chips):
            copy(1 + j, (*chip, c), me).wait_recv()
            passed[j].start()
        copy(0, sibling, me).wait_recv()
        for j, chip in enumerate(chips):
            copy(4 + j, (*chip, 1 - c), me).wait_recv()
        for cp in first + passed:
            cp.wait_send()
        mine.wait()

    return pl.pallas_call(
        body, name=name,
        out_shape=jax.ShapeDtypeStruct((N_DEV * m_per, w), v.dtype),
        in_specs=[pl.BlockSpec(memory_space=pltpu.VMEM)], out_specs=pl.BlockSpec(memory_space=pltpu.VMEM),
        scratch_shapes=[pltpu.SemaphoreType.DMA((7,)), pltpu.SemaphoreType.DMA((7,)), pltpu.SemaphoreType.DMA],
    )(v)


def _sum_blocks(a, n, name):
    m_per, w = a.shape[0] // n, a.shape[1]

    def body(a_ref, o_ref):
        s = a_ref[0:m_per, :]
        for j in range(1, n):
            s = s + a_ref[j * m_per:(j + 1) * m_per, :]
        o_ref[...] = s

    return pl.pallas_call(body, name=name, out_shape=jax.ShapeDtypeStruct((m_per, w), a.dtype))(a)


def _pack_gains(d):
    flat = jnp.concatenate([d[n].reshape(1, -1) for n in GAIN_NAMES], axis=1)
    n = flat.shape[-1]
    rows = -(-n // (PACK_W * 8)) * 8
    return jnp.pad(flat, ((0, 0), (0, rows * PACK_W - n))).reshape(rows, PACK_W)


def _to_slabs(g):
    a, nb = g.shape
    return jnp.transpose(g.reshape(a, N_CHIPS, nb // N_CHIPS), (1, 0, 2))


def _from_slabs(s):
    n, a, b = s.shape
    return jnp.transpose(s, (1, 0, 2)).reshape(a, n * b)


def _unpack_gains(slab, shapes):
    flat = slab.reshape(-1)
    out, off = {}, 0
    for n in GAIN_NAMES:
        a, b = shapes[n]
        out[n] = flat[off:off + a * b].reshape(a, b)
        off += a * b
    return out


def _reduce_scatter(gs):
    got = _swap_halves(gs, "rs_swap_halves")
    sums = [_add_pair(g, h, "rs_add_pair") for g, h in zip(gs, got)]
    got = _exchange_chips(sums, "rs_exchange_chips")
    halves = [_add_chips(s, h, "rs_add_chips") for s, h in zip(sums, got)]
    return [b.reshape(b.shape[1] * 2, b.shape[2]) for b in _share_halves(halves, "rs_share_halves")]


def _ffn_forward(h, wl, pre, tag):
    g, u, a = _ffn_up(h, wl[pre + '_w_gate'], wl[pre + '_w_up'], "ffn_up" + tag)
    wd = wl[pre + '_w_down']
    z = _mm([(a, wd.reshape(-1, wd.shape[2]))], 'nn', F32, "ffn_down" + tag)
    return z, (g, u, a)


def _ffn_backward(dz, h, saved, wl, pre, grads, tag):
    g, u, a = saved
    wd = wl[pre + '_w_down']
    dg, du = _ffn_bwd_act(dz, wd.reshape(-1, wd.shape[2]), g, u, "ffn_bwd_act" + tag)
    grads[pre + '_w_down'] = _mm([(a, dz)], 'tn', BF16, "ffn_dw_down" + tag, tm=wd.shape[1], tn=1024, tk=1024).reshape(wd.shape)
    grads[pre + '_w_gate'] = _mm([(h, dg)], 'tn', BF16, "ffn_dw_gate" + tag, tk=1024, out_slabs=N_CHIPS)
    grads[pre + '_w_up'] = _mm([(h, du)], 'tn', BF16, "ffn_dw_up" + tag, tk=1024, out_slabs=N_CHIPS)
    return _mm([(dg, wl[pre + '_w_gate']), (du, wl[pre + '_w_up'])], 'nt', F32, "ffn_dh" + tag, tn=1024, b_slabs=True)


def kernel(x, positions, ffn1_pre_g, ffn1_post_g, ffn1_w_gate, ffn1_w_up, ffn1_w_down, mix_pre_g, mix_post_g, w_in, mla_q_norm_g, mla_w_uq, mla_kv_norm_g, mla_w_ukv, w_o, ffn2_pre_g, ffn2_post_g, ffn2_w_gate, ffn2_w_up, ffn2_w_down, loss_target, m_ffn1_pre_g, m_ffn1_post_g, m_ffn1_w_gate, m_ffn1_w_up, m_ffn1_w_down, m_mix_pre_g, m_mix_post_g, m_w_in, m_mla_q_norm_g, m_mla_w_uq, m_mla_kv_norm_g, m_mla_w_ukv, m_w_o, m_ffn2_pre_g, m_ffn2_post_g, m_ffn2_w_gate, m_ffn2_w_up, m_ffn2_w_down, v_ffn1_pre_g, v_ffn1_post_g, v_ffn1_w_gate, v_ffn1_w_up, v_ffn1_w_down, v_mix_pre_g, v_mix_post_g, v_w_in, v_mla_q_norm_g, v_mla_w_uq, v_mla_kv_norm_g, v_mla_w_ukv, v_w_o, v_ffn2_pre_g, v_ffn2_post_g, v_ffn2_w_gate, v_ffn2_w_up, v_ffn2_w_down):
    given = dict(locals())
    ws = {n: given[n] for n in WEIGHT_NAMES}
    ms = {n: given['m_' + n] for n in WEIGHT_NAMES}
    vs = {n: given['v_' + n] for n in WEIGHT_NAMES}

    depth = ffn1_pre_g.shape[0]
    t, d_model = x.shape[1], x.shape[2]
    rq, rkv = mla_q_norm_g.shape[1], mla_kv_norm_g.shape[1]
    h_mla = mla_w_uq.shape[2] * N_CHIPS // MLA_QK
    wd_ = (w_in.shape[2] * N_CHIPS - rq - rkv - MLA_ROPE) // 3
    assert rq == rkv and (rq + rkv) % wd_ == 0 and wd_ % HEAD_DIM == 0
    assert mla_w_ukv.shape[2] * N_CHIPS == h_mla * 256 and w_o.shape[1] * N_CHIPS == h_mla * MLA_V + wd_
    lat = rq + rkv
    in_p = lat + 3 * wd_ + LANES
    qd_block = lat // wd_
    kr_block = (lat + 3 * wd_) // LANES
    gain_shapes = {n: ws[n].shape for n in GAIN_NAMES}

    pos = positions[0].astype(F32)[:, None]

    def cos_sin(dim):
        inv = ROPE_THETA ** (-jnp.arange(0, dim, 2, dtype=F32) / dim)
        return jnp.cos(pos * inv), jnp.sin(pos * inv)

    cos_a, sin_a = cos_sin(MLA_ROPE)
    cos_p, sin_p = cos_sin(PARTIAL_ROPE)
    tab_q = [_rot_tables(cos_a, sin_a, MLA_QK_PAD, MLA_NOPE, f) for f in (True, False)]
    tab_kr = [_rot_tables(cos_a, sin_a, LANES, 0, f) for f in (True, False)]
    tab_d = [_rot_tables(cos_p, sin_p, HEAD_DIM, 0, f) for f in (True, False)]
    s_a, s_p = MLA_ROPE // 2, PARTIAL_ROPE // 2

    w16 = {n: ws[n].astype(BF16) for n in MATRIX_NAMES}
    layers = []
    for l in range(depth):
        wl = dict(zip(MATRIX_NAMES, _allgather_shards([w16[n][l] for n in MATRIX_NAMES], "allgather_weights")))
        w_in_l = _from_slabs(wl['w_in'])
        wl['w_in'] = jnp.concatenate([w_in_l[:, :lat], w_in_l[:, lat + MLA_ROPE:], w_in_l[:, lat:lat + MLA_ROPE],
                                      jnp.zeros((d_model, LANES - MLA_ROPE), BF16)], axis=1)
        wl['mla_w_uq'] = jnp.pad(_from_slabs(wl['mla_w_uq']).reshape(rq, h_mla, MLA_QK),
                                 ((0, 0), (0, 0), (0, MLA_QK_PAD - MLA_QK))).reshape(rq, h_mla * MLA_QK_PAD)
        wl['w_o'] = wl['w_o'].reshape(-1, d_model)
        layers.append(wl)

    xc = x[0]
    h = _rms_fwd(xc, ffn1_pre_g[0], "rms_first")
    saved = []
    dy = loss = None
    for l in range(depth):
        wl, sv = layers[l], {}
        sv['x0'], sv['h1'] = xc, h
        z, sv['ffn1'] = _ffn_forward(h, wl, 'ffn1', "_1")
        sv['z1'] = z
        xc, h = _post_pre(xc, z, ffn1_post_g[l], 0.5, mix_pre_g[l], "post_pre_1")
        sv['x1'], sv['h2'] = xc, h

        proj = _mm([(h, wl['w_in'])], 'nn', BF16, "mix_in", tn=1408, tk=d_model)
        cqn = _rms_fwd(proj, mla_q_norm_g[l], "rms_cq", col_block=0)
        ckvn = _rms_fwd(proj, mla_kv_norm_g[l], "rms_ckv", col_block=1)
        qf = _rot([_mm([(cqn, wl['mla_w_uq'])], 'nn', F32, "mla_uq")], tab_q[0], s_a, "rope_q")
        kv = _mm([(ckvn, wl['mla_w_ukv'])], 'nn', BF16, "mla_ukv", b_slabs=True)
        kf = _mla_kfull(kv, proj, kr_block, tab_kr[0], h_mla, "mla_kfull")
        oa, lse = _flash_fwd(qf, kf, kv, h_mla, "mla_attn")
        qd = _rot([proj], tab_d[0], s_p, "rope_qd", col_block=qd_block, width=wd_)
        kd = _rot([proj], tab_d[0], s_p, "rope_kd", col_block=qd_block + 1, width=wd_)
        vd = proj[:, lat + 2 * wd_:lat + 3 * wd_]
        outs = [_dil_fwd(qd, kd, vd, dil, "dil_attn_%d" % dil) for _, dil in DIL_PATTERNS]
        ob, mt = _dil_combine([o for o, _ in outs], [s for _, s in outs], "dil_combine")
        o = jnp.concatenate([oa, ob], axis=1)
        z = _mm([(o, wl['w_o'])], 'nn', F32, "mix_out")
        sv.update(proj=proj, cqn=cqn, ckvn=ckvn, qf=qf, kv=kv, kf=kf, oa=oa, lse=lse, qd=qd, kd=kd, vd=vd, ob=ob,
                  mt=mt, o=o, z2=z)
        xc, h = _post_pre(xc, z, mix_post_g[l], 1.0, ffn2_pre_g[l], "post_pre_2")
        sv['x2'], sv['h3'] = xc, h

        z, sv['ffn2'] = _ffn_forward(h, wl, 'ffn2', "_2")
        sv['z3'] = z
        if l + 1 < depth:
            xc, h = _post_pre(xc, z, ffn2_post_g[l], 0.5, ffn1_pre_g[l + 1], "post_pre_3")
        else:
            dy, loss = _post_loss(xc, z, ffn2_post_g[l], 0.5, loss_target[0], "post_loss")
        saved.append(sv)

    dx = dy
    gain_grads = {n: [None] * depth for n in GAIN_NAMES}
    shard_grads = [None] * depth
    for l in reversed(range(depth)):
        wl, sv, grads = layers[l], saved[l], {}

        dz, gain_grads['ffn2_post_g'][l] = _rms_bwd(sv['z3'], ffn2_post_g[l], dx, 0.5, "rms_bwd_post_f", out_dtype=BF16)
        dh = _ffn_backward(dz, sv['h3'], sv['ffn2'], wl, 'ffn2', grads, "_2")
        dx, gain_grads['ffn2_pre_g'][l] = _rms_bwd(sv['x2'], ffn2_pre_g[l], dh, 1.0, "rms_bwd_pre", res=dx)

        dz, gain_grads['mix_post_g'][l] = _rms_bwd(sv['z2'], mix_post_g[l], dx, 1.0, "rms_bwd_post_m", out_dtype=BF16)
        do = _mm([(dz, wl['w_o'])], 'nt', BF16, "mix_out_dx", tk=d_model)
        grads['w_o'] = _mm([(sv['o'], dz)], 'tn', BF16, "mix_out_dw", tn=1024, tk=1024).reshape(N_CHIPS, -1, d_model)
        do_a, do_b = do[:, :h_mla * MLA_V], do[:, h_mla * MLA_V:]

        delta = _delta(do_a, sv['oa'], "delta_a")
        dqf = _flash_bwd_dq(sv['qf'], sv['kf'], sv['kv'], do_a, sv['lse'], delta, h_mla, "mla_attn_dq")
        dkf, dva = _flash_bwd_dkv(sv['qf'], sv['kf'], sv['kv'], do_a, _stat_rows(sv['lse'], h_mla),
                                  _stat_rows(delta, h_mla), h_mla, "mla_attn_dkv")
        dq = _rot([dqf], tab_q[1], s_a, "rope_q_bwd")
        grads['mla_w_uq'] = _mm([(sv['cqn'], dq)], 'tn', BF16, "mla_uq_dw", tk=1024)
        dcqn = _mm([(dq, wl['mla_w_uq'])], 'nt', F32, "mla_uq_dx", tk=1024)
        dcq, gain_grads['mla_q_norm_g'][l] = _rms_bwd(sv['proj'], mla_q_norm_g[l], dcqn, 1.0, "rms_bwd_cq",
                                                      col_block=0, out_dtype=BF16)
        dkv, dkr = _mla_dkv(dkf, dva, tab_kr[1], h_mla, "mla_dkv")
        grads['mla_w_ukv'] = _mm([(sv['ckvn'], dkv)], 'tn', BF16, "mla_ukv_dw", tk=1024, out_slabs=N_CHIPS)
        dckvn = _mm([(dkv, wl['mla_w_ukv'])], 'nt', F32, "mla_ukv_dx", b_slabs=True)
        dckv, gain_grads['mla_kv_norm_g'][l] = _rms_bwd(sv['proj'], mla_kv_norm_g[l], dckvn, 1.0, "rms_bwd_ckv",
                                                        col_block=1, out_dtype=BF16)

        delta = _delta(do_b, sv['ob'], "delta_b")
        dqs, dks, dvs = [], [], []
        for _, dil in DIL_PATTERNS:
            dqs.append(_dil_bwd_dq(sv['qd'], sv['kd'], sv['vd'], do_b, sv['mt'], delta, dil, "dil_attn_dq_%d" % dil))
            dk_, dv_ = _dil_bwd_dkv(sv['qd'], sv['kd'], sv['vd'], do_b, sv['mt'], delta, dil, "dil_attn_dkv_%d" % dil)
            dks.append(dk_)
            dvs.append(dv_)
        dqd = _rot(dqs, tab_d[1], s_p, "rope_qd_bwd")
        dkd = _rot(dks, tab_d[1], s_p, "rope_kd_bwd")
        dvd = _rot(dvs, None, 0, "sum_dvd")
        dproj = jnp.concatenate([dcq, dckv, dqd, dkd, dvd, dkr], axis=1)
        grads['w_in'] = _mm([(sv['h2'], dproj)], 'tn', BF16, "mix_in_dw", tn=1408, tk=1024)
        dh = _mm([(dproj, wl['w_in'])], 'nt', F32, "mix_in_dx", tn=1024, tk=1408)
        dx, gain_grads['mix_pre_g'][l] = _rms_bwd(sv['x1'], mix_pre_g[l], dh, 1.0, "rms_bwd_pre", res=dx)

        dz, gain_grads['ffn1_post_g'][l] = _rms_bwd(sv['z1'], ffn1_post_g[l], dx, 0.5, "rms_bwd_post_f", out_dtype=BF16)
        dh = _ffn_backward(dz, sv['h1'], sv['ffn1'], wl, 'ffn1', grads, "_1")
        dx, gain_grads['ffn1_pre_g'][l] = _rms_bwd(sv['x0'], ffn1_pre_g[l], dh, 1.0, "rms_bwd_pre", res=dx)

        gp = grads['w_in']
        grads['w_in'] = _to_slabs(jnp.concatenate(
            [gp[:, :lat], gp[:, in_p - LANES:in_p - LANES + MLA_ROPE], gp[:, lat:lat + 3 * wd_]], axis=1))
        grads['mla_w_uq'] = _to_slabs(
            grads['mla_w_uq'].reshape(rq, h_mla, MLA_QK_PAD)[:, :, :MLA_QK].reshape(rq, h_mla * MLA_QK))
        shard_grads[l] = dict(zip(MATRIX_NAMES, _reduce_scatter([grads[n] for n in MATRIX_NAMES])))

    gg = _pack_gains({n: jnp.stack(gain_grads[n]) for n in GAIN_NAMES})
    gg = _sum_blocks(_allgather_small(gg, "allgather_gain_grads"), N_DEV, "sum_gain_grads")

    grad_w, delta_w, new_m, new_v = {}, {}, {}, {}
    dg, mg, vg = _adamw(_pack_gains(ws), gg, _pack_gains(ms), _pack_gains(vs), "adamw_gains")
    for dst, slab in ((grad_w, gg), (delta_w, dg), (new_m, mg), (new_v, vg)):
        dst.update(_unpack_gains(slab, gain_shapes))
    for n in MATRIX_NAMES:
        shape = ws[n].shape
        g = jnp.stack([shard_grads[l][n] for l in range(depth)])
        flat = lambda a: a.reshape(shape[0] * shape[1], shape[2])
        d_, m_, v_ = _adamw(flat(ws[n]), flat(g), flat(ms[n]), flat(vs[n]), "adamw_" + n)
        grad_w[n], delta_w[n], new_m[n], new_v[n] = g, d_.reshape(shape), m_.reshape(shape), v_.reshape(shape)

    loss = lax.psum(loss[0, 0], ("x", "y", "c"))
    return (loss, dx[None], *[grad_w[n] for n in WEIGHT_NAMES], *[delta_w[n] for n in WEIGHT_NAMES],
            *[new_m[n] for n in WEIGHT_NAMES], *[new_v[n] for n in WEIGHT_NAMES])
```

```python
import functools
import math

import jax
import jax.numpy as jnp
from jax import lax
from jax.experimental import pallas as pl
from jax.experimental.pallas import tpu as pltpu

F32 = jnp.float32
BF16 = jnp.bfloat16

HEAD_DIM = 128
MLA_NOPE = 128
MLA_ROPE = 64
MLA_V = 128
MLA_QK = MLA_NOPE + MLA_ROPE
MLA_QK_PAD = 256
DIL_PATTERNS = ((128, 1), (512, 4), (2048, 16))
DIL_BLK = 128
PARTIAL_ROPE = HEAD_DIM // 4
ROPE_THETA = 500000.0
RMS_EPS = 1e-6
NEG = -1e30
ADAM_LR, ADAM_B1, ADAM_B2, ADAM_EPS, ADAM_WD, ADAM_STEP = 0.001, 0.9, 0.999, 1e-08, 0.01, 10

N_CHIPS = 4
N_DEV = 8
LANES = 128
PACK_W = 1024
VMEM_LIMIT = 56 * 1024 * 1024
MESH = pl.DeviceIdType.MESH

WEIGHT_NAMES = ('ffn1_pre_g', 'ffn1_post_g', 'ffn1_w_gate', 'ffn1_w_up', 'ffn1_w_down', 'mix_pre_g', 'mix_post_g',
                'w_in', 'mla_q_norm_g', 'mla_w_uq', 'mla_kv_norm_g', 'mla_w_ukv', 'w_o', 'ffn2_pre_g', 'ffn2_post_g',
                'ffn2_w_gate', 'ffn2_w_up', 'ffn2_w_down')
GAIN_NAMES = tuple(n for n in WEIGHT_NAMES if n.endswith('_g'))
MATRIX_NAMES = tuple(n for n in WEIGHT_NAMES if not n.endswith('_g'))


def _tile(dim, pref, mult=LANES):
    if dim <= pref:
        return dim
    t = (pref // mult) * mult
    while t >= mult:
        if dim % t == 0:
            return t
        t -= mult
    return dim


def _params(*sem):
    return pltpu.CompilerParams(dimension_semantics=sem, vmem_limit_bytes=VMEM_LIMIT)


def _dot(a, b, dims):
    return lax.dot_general(a.astype(BF16), b.astype(BF16), (dims, ((), ())), preferred_element_type=F32)


_NN = ((1,), (0,))
_NT = ((1,), (1,))
_TN = ((0,), (0,))


def _mm(pairs, mode, out_dtype, name, tm=1024, tn=512, tk=512, b_slabs=False, out_slabs=1):
    a0, b0 = pairs[0]
    if mode == 'nn':
        m, k = a0.shape
        n = b0.shape[0] * b0.shape[2] if b_slabs else b0.shape[1]
        tn = b0.shape[2] if b_slabs else tn
    elif mode == 'nt':
        m, k = a0.shape
        n = b0.shape[1] if b_slabs else b0.shape[0]
        tk = b0.shape[2] if b_slabs else tk
    else:
        (k, m), n = a0.shape, b0.shape[1]
        tn = n // out_slabs if out_slabs > 1 else tn
    tm, tn, tk = _tile(m, tm), _tile(n, tn), _tile(k, tk)
    nk = k // tk
    dims = {'nn': _NN, 'nt': _NT, 'tn': _TN}[mode]
    if mode == 'tn':
        a_spec = pl.BlockSpec((tk, tm), lambda i, j, kk: (kk, i))
    else:
        a_spec = pl.BlockSpec((tm, tk), lambda i, j, kk: (i, kk))
    if mode == 'nt':
        b_spec = (pl.BlockSpec((None, tn, tk), lambda i, j, kk: (kk, j, 0)) if b_slabs
                  else pl.BlockSpec((tn, tk), lambda i, j, kk: (j, kk)))
    else:
        b_spec = (pl.BlockSpec((None, tk, tn), lambda i, j, kk: (j, kk, 0)) if b_slabs
                  else pl.BlockSpec((tk, tn), lambda i, j, kk: (kk, j)))
    if out_slabs > 1:
        out_spec = pl.BlockSpec((None, tm, tn), lambda i, j, kk: (j, i, 0))
        out_shape = jax.ShapeDtypeStruct((out_slabs, m, tn), out_dtype)
    else:
        out_spec = pl.BlockSpec((tm, tn), lambda i, j, kk: (i, j))
        out_shape = jax.ShapeDtypeStruct((m, n), out_dtype)
    n_pairs = len(pairs)

    def body(*refs):
        o_ref = refs[2 * n_pairs]

        def partial_sum():
            s = _dot(refs[0][...], refs[1][...], dims)
            for p in range(1, n_pairs):
                s = s + _dot(refs[2 * p][...], refs[2 * p + 1][...], dims)
            return s

        if nk == 1:
            o_ref[...] = partial_sum().astype(out_dtype)
        else:
            acc = refs[2 * n_pairs + 1]
            kk = pl.program_id(2)

            @pl.when(kk == 0)
            def _():
                acc[...] = jnp.zeros_like(acc)

            acc[...] += partial_sum()

            @pl.when(kk == nk - 1)
            def _():
                o_ref[...] = acc[...].astype(out_dtype)

    return pl.pallas_call(
        body, name=name, grid=(m // tm, n // tn, nk),
        in_specs=[a_spec, b_spec] * n_pairs,
        out_specs=out_spec, out_shape=out_shape,
        scratch_shapes=[pltpu.VMEM((tm, tn), F32)] if nk > 1 else [],
        compiler_params=_params("parallel", "parallel", "arbitrary"),
    )(*[t for pair in pairs for t in pair])


def _ffn_up(h, wg, wu, name, job=None):
    m, d = h.shape
    n_slabs, _, fs = wg.shape
    tm = _tile(m, 512)

    def body(h_ref, wg_ref, wu_ref, g_ref, u_ref, a_ref):
        g = _dot(h_ref[...], wg_ref[...], _NN)
        u = _dot(h_ref[...], wu_ref[...], _NN)
        g_ref[...] = g.astype(BF16)
        u_ref[...] = u.astype(BF16)
        a_ref[...] = (g * jax.nn.sigmoid(g) * u).astype(BF16)

    w_spec = pl.BlockSpec((None, d, fs), lambda j, i: (j, 0, 0))
    o_spec = pl.BlockSpec((tm, fs), lambda j, i: (i, j))
    shape = jax.ShapeDtypeStruct((m, n_slabs * fs), BF16)
    return _carry_call(job, body, name, (n_slabs, m // tm),
                       [pl.BlockSpec((tm, d), lambda j, i: (i, 0)), w_spec, w_spec], [o_spec] * 3, [shape] * 3, [],
                       ("parallel", "arbitrary"), (h, wg, wu))


def _ffn_bwd_act(dz, wd, g, u, name):
    m, d = dz.shape
    f = wd.shape[0]
    tm, tn = _tile(m, 1024), _tile(f, 512)

    def body(dz_ref, wd_ref, g_ref, u_ref, dg_ref, du_ref):
        da = _dot(dz_ref[...], wd_ref[...], _NT)
        gg = g_ref[...].astype(F32)
        uu = u_ref[...].astype(F32)
        sg = jax.nn.sigmoid(gg)
        du_ref[...] = (da * (gg * sg)).astype(BF16)
        dg_ref[...] = (da * uu * (sg * (1.0 + gg * (1.0 - sg)))).astype(BF16)

    t_spec = pl.BlockSpec((tm, tn), lambda i, j: (i, j))
    shape = jax.ShapeDtypeStruct((m, f), BF16)
    return pl.pallas_call(
        body, name=name, grid=(m // tm, f // tn),
        in_specs=[pl.BlockSpec((tm, d), lambda i, j: (i, 0)), pl.BlockSpec((tn, d), lambda i, j: (j, 0)), t_spec, t_spec],
        out_specs=[t_spec] * 2, out_shape=[shape] * 2,
        compiler_params=_params("parallel", "arbitrary"),
    )(dz, wd, g, u)


def _rms(x, g):
    r = lax.rsqrt(jnp.mean(x * x, axis=-1, keepdims=True) + RMS_EPS)
    return x * r * g


def _rms_fwd(x, g, name, col_block=0, out_dtype=BF16):
    t = x.shape[0]
    w = g.shape[-1]
    tr = _tile(t, 256)

    def body(x_ref, g_ref, o_ref):
        o_ref[...] = _rms(x_ref[...].astype(F32), g_ref[...]).astype(out_dtype)

    return pl.pallas_call(
        body, name=name, grid=(t // tr,),
        in_specs=[pl.BlockSpec((tr, w), lambda i: (i, col_block)), pl.BlockSpec((1, w), lambda i: (0, 0))],
        out_specs=pl.BlockSpec((tr, w), lambda i: (i, 0)),
        out_shape=jax.ShapeDtypeStruct((t, w), out_dtype),
        compiler_params=_params("parallel"),
    )(x, g.reshape(1, w))


def _post_pre(x, z, g_post, alpha, g_next, name):
    t, w = x.shape
    tr = _tile(t, 256)

    def body(x_ref, z_ref, gp_ref, gn_ref, xo_ref, h_ref):
        xn = x_ref[...] + alpha * _rms(z_ref[...], gp_ref[...])
        xo_ref[...] = xn
        h_ref[...] = _rms(xn, gn_ref[...]).astype(BF16)

    row = pl.BlockSpec((tr, w), lambda i: (i, 0))
    vec = pl.BlockSpec((1, w), lambda i: (0, 0))
    return pl.pallas_call(
        body, name=name, grid=(t // tr,),
        in_specs=[row, row, vec, vec], out_specs=[row, row],
        out_shape=[jax.ShapeDtypeStruct((t, w), F32), jax.ShapeDtypeStruct((t, w), BF16)],
        compiler_params=_params("parallel"),
    )(x, z, g_post.reshape(1, w), g_next.reshape(1, w))


def _post_loss(x, z, g_post, alpha, target, name):
    t, w = x.shape
    tr = _tile(t, 256)

    def body(x_ref, z_ref, gp_ref, t_ref, dy_ref, loss_ref):
        err = x_ref[...] + alpha * _rms(z_ref[...], gp_ref[...]) - t_ref[...]
        dy_ref[...] = err * (1.0 / w)

        @pl.when(pl.program_id(0) == 0)
        def _():
            loss_ref[...] = jnp.zeros_like(loss_ref)

        loss_ref[...] += jnp.sum(jnp.mean(err * err, axis=-1, keepdims=True), axis=0, keepdims=True) * 0.5

    row = pl.BlockSpec((tr, w), lambda i: (i, 0))
    vec = pl.BlockSpec((1, w), lambda i: (0, 0))
    return pl.pallas_call(
        body, name=name, grid=(t // tr,),
        in_specs=[row, row, vec, row], out_specs=[row, pl.BlockSpec((1, 1), lambda i: (0, 0))],
        out_shape=[jax.ShapeDtypeStruct((t, w), F32), jax.ShapeDtypeStruct((1, 1), F32)],
        compiler_params=_params("arbitrary"),
    )(x, z, g_post.reshape(1, w), target)


def _rms_bwd(x, g, dy, alpha, name, res=None, col_block=0, out_dtype=F32):
    t = x.shape[0]
    w = g.shape[-1]
    tr = _tile(t, 256)
    has_res = res is not None

    def body(*refs):
        x_ref, g_ref, dy_ref = refs[:3]
        dx_ref, dg_ref = refs[-2:]
        xx = x_ref[...].astype(F32)
        dyy = dy_ref[...].astype(F32) * alpha
        r = lax.rsqrt(jnp.mean(xx * xx, axis=-1, keepdims=True) + RMS_EPS)
        xh = xx * r
        gy = dyy * g_ref[...]
        dx = r * (gy - xh * jnp.mean(gy * xh, axis=-1, keepdims=True))
        if has_res:
            dx = dx + refs[3][...]
        dx_ref[...] = dx.astype(out_dtype)

        @pl.when(pl.program_id(0) == 0)
        def _():
            dg_ref[...] = jnp.zeros_like(dg_ref)

        dg_ref[...] += jnp.sum(dyy * xh, axis=0, keepdims=True)

    row = pl.BlockSpec((tr, w), lambda i: (i, 0))
    vec = pl.BlockSpec((1, w), lambda i: (0, 0))
    ins = [x, g.reshape(1, w), dy] + ([res] if has_res else [])
    dx, dg = pl.pallas_call(
        body, name=name, grid=(t // tr,),
        in_specs=[pl.BlockSpec((tr, w), lambda i: (i, col_block)), vec, row] + ([row] if has_res else []),
        out_specs=[row, vec],
        out_shape=[jax.ShapeDtypeStruct((t, w), out_dtype), jax.ShapeDtypeStruct((1, w), F32)],
        compiler_params=_params("arbitrary"),
    )(*ins)
    return dx, dg.reshape(w)


def _rot_tables(cos, sin, period, start, fwd):
    t, s = cos.shape
    one = jnp.ones((t, start), F32)
    tail_w = period - start - 2 * s
    tail = jnp.ones((t, tail_w), F32) if start == 0 else jnp.zeros((t, tail_w), F32)
    z = lambda w: jnp.zeros((t, w), F32)
    c = jnp.concatenate([one, cos, cos, tail], axis=1)
    sm = jnp.concatenate([z(start + s), sin, z(tail_w)], axis=1)
    sp = jnp.concatenate([z(start), -sin, z(s + tail_w)], axis=1)
    if fwd:
        return c, sm, sp
    return c, jnp.roll(sp, s, axis=1), jnp.roll(sm, -s, axis=1)


def _rot_apply(x, c, sm, sp, shift):
    w = x.shape[-1]
    return x * c + pltpu.roll(x, shift, 1) * sm + pltpu.roll(x, w - shift, 1) * sp


def _rot(xs, tables, shift, name, col_block=0, width=None, out_dtype=BF16):
    t = xs[0].shape[0]
    w = width if width is not None else xs[0].shape[1]
    tr = _tile(t, 256)
    n_in = len(xs)
    period = tables[0].shape[1] if tables is not None else w

    def body(*refs):
        o_ref = refs[-1]
        for hh in range(w // period):
            sl = slice(hh * period, (hh + 1) * period)
            v = refs[0][:, sl].astype(F32)
            for p in range(1, n_in):
                v = v + refs[p][:, sl].astype(F32)
            if tables is not None:
                c_ref, sm_ref, sp_ref = refs[n_in:n_in + 3]
                v = _rot_apply(v, c_ref[...], sm_ref[...], sp_ref[...], shift)
            o_ref[:, sl] = v.astype(out_dtype)

    tab = pl.BlockSpec((tr, period), lambda i: (i, 0))
    return pl.pallas_call(
        body, name=name, grid=(t // tr,),
        in_specs=[pl.BlockSpec((tr, w), lambda i: (i, col_block))] * n_in + ([tab] * 3 if tables is not None else []),
        out_specs=pl.BlockSpec((tr, w), lambda i: (i, 0)),
        out_shape=jax.ShapeDtypeStruct((t, w), out_dtype),
        compiler_params=_params("parallel"),
    )(*xs, *(tables if tables is not None else ()))


def _mla_kfull(kv, proj, kr_block, tables, heads, name):
    t = kv.shape[0]
    tr = _tile(t, 256)

    def body(kv_ref, kr_ref, c_ref, sm_ref, sp_ref, o_ref):
        kr = _rot_apply(kr_ref[...].astype(F32), c_ref[...], sm_ref[...], sp_ref[...], MLA_ROPE // 2).astype(BF16)
        for hh in range(heads):
            o_ref[:, hh * 256:hh * 256 + 128] = kv_ref[:, hh * 256:hh * 256 + 128]
            o_ref[:, hh * 256 + 128:(hh + 1) * 256] = kr

    tab = pl.BlockSpec((tr, LANES), lambda i: (i, 0))
    full = pl.BlockSpec((tr, heads * 256), lambda i: (i, 0))
    return pl.pallas_call(
        body, name=name, grid=(t // tr,),
        in_specs=[full, pl.BlockSpec((tr, LANES), lambda i: (i, kr_block)), tab, tab, tab],
        out_specs=full, out_shape=jax.ShapeDtypeStruct((t, heads * 256), BF16),
        compiler_params=_params("parallel"),
    )(kv, proj, *tables)


def _mla_dkv(dk, dv, tables, heads, name):
    t = dk.shape[0]
    tr = _tile(t, 256)

    def body(dk_ref, dv_ref, c_ref, sm_ref, sp_ref, dkv_ref, dkr_ref):
        acc = jnp.zeros((tr, LANES), F32)
        for hh in range(heads):
            dkv_ref[:, hh * 256:hh * 256 + 128] = dk_ref[:, hh * 256:hh * 256 + 128].astype(BF16)
            dkv_ref[:, hh * 256 + 128:(hh + 1) * 256] = dv_ref[:, hh * 128:(hh + 1) * 128].astype(BF16)
            acc = acc + dk_ref[:, hh * 256 + 128:(hh + 1) * 256]
        dkr_ref[...] = _rot_apply(acc, c_ref[...], sm_ref[...], sp_ref[...], MLA_ROPE // 2).astype(BF16)

    tab = pl.BlockSpec((tr, LANES), lambda i: (i, 0))
    full = pl.BlockSpec((tr, heads * 256), lambda i: (i, 0))
    return pl.pallas_call(
        body, name=name, grid=(t // tr,),
        in_specs=[full, pl.BlockSpec((tr, heads * 128), lambda i: (i, 0)), tab, tab, tab],
        out_specs=[full, tab],
        out_shape=[jax.ShapeDtypeStruct((t, heads * 256), BF16), jax.ShapeDtypeStruct((t, LANES), BF16)],
        compiler_params=_params("parallel"),
    )(dk, dv, *tables)


def _delta(do, o, name):
    t, w = do.shape
    tr = _tile(t, 256)

    def body(do_ref, o_ref, d_ref):
        for hh in range(w // HEAD_DIM):
            sl = slice(hh * HEAD_DIM, (hh + 1) * HEAD_DIM)
            s = jnp.sum(do_ref[:, sl].astype(F32) * o_ref[:, sl].astype(F32), axis=1, keepdims=True)
            d_ref[:, sl] = jnp.broadcast_to(s, (tr, HEAD_DIM))

    row = pl.BlockSpec((tr, w), lambda i: (i, 0))
    return pl.pallas_call(
        body, name=name, grid=(t // tr,), in_specs=[row, row], out_specs=row,
        out_shape=jax.ShapeDtypeStruct((t, w), F32), compiler_params=_params("parallel"),
    )(do, o)


def _causal_mask(s, row0, col0):
    qi = row0 + lax.broadcasted_iota(jnp.int32, s.shape, 0)
    kj = col0 + lax.broadcasted_iota(jnp.int32, s.shape, 1)
    return jnp.where(kj <= qi, s, NEG)


def _head_group(heads):
    return 2 if heads % 2 == 0 else 1


def _v_specs(rows, hp, row_block):
    return [pl.BlockSpec((rows, MLA_V), functools.partial(lambda h, i, hh: (row_block(i), 2 * (h * hp + hh) + 1), hh=hh))
            for hh in range(hp)]


def _flash_fwd(q, k, kv, heads, name, job=None):
    t = q.shape[0]
    tq = _tile(t, 512)
    hp = _head_group(heads)
    nrep = tq // HEAD_DIM
    scale = 1.0 / math.sqrt(MLA_QK)

    def body(q_ref, k_ref, *rest):
        v_refs = rest[:hp]
        o_ref, lse_ref, m_sc, acc_sc = rest[hp:]
        i = pl.program_id(1)
        m_sc[...] = jnp.full_like(m_sc, NEG)
        acc_sc[...] = jnp.zeros_like(acc_sc)
        ones = jnp.ones((tq, HEAD_DIM), BF16)

        def step(j, masked):
            rows = pl.ds(pl.multiple_of(j * tq, tq), tq)
            for hh in range(hp):
                hs = slice(hh * 256, (hh + 1) * 256)
                s = _dot(q_ref[:, hs], k_ref[rows, hs], _NT) * scale
                if masked:
                    s = _causal_mask(s, 0, 0)
                m_prev = m_sc[hh]
                m_new = jnp.maximum(m_prev, jnp.broadcast_to(jnp.max(s, axis=1, keepdims=True), (tq, HEAD_DIM)))
                a = jnp.exp(m_prev - m_new)
                p = jnp.exp(s - jnp.tile(m_new, (1, nrep)))
                pv = _dot(p, jnp.concatenate([v_refs[hh][rows, :], ones], axis=1), _NN)
                acc_sc[hh] = jnp.tile(a, (1, 2)) * acc_sc[hh] + pv
                m_sc[hh] = m_new

        def loop_body(j, carry):
            step(j, False)
            return carry

        lax.fori_loop(0, i, loop_body, 0)
        step(i, True)
        for hh in range(hp):
            l = acc_sc[hh][:, HEAD_DIM:]
            o_ref[:, hh * MLA_V:(hh + 1) * MLA_V] = (acc_sc[hh][:, :HEAD_DIM] / l).astype(BF16)
            lse_ref[:, hh * HEAD_DIM:(hh + 1) * HEAD_DIM] = m_sc[hh] + jnp.log(l)

    stat = pl.BlockSpec((tq, hp * HEAD_DIM), lambda h, i: (i, h))
    return _carry_call(
        job, body, name, (heads // hp, t // tq),
        [pl.BlockSpec((tq, hp * 256), lambda h, i: (i, h)), pl.BlockSpec((t, hp * 256), lambda h, i: (0, h))]
        + _v_specs(t, hp, lambda i: 0), [stat, stat],
        [jax.ShapeDtypeStruct((t, heads * MLA_V), BF16), jax.ShapeDtypeStruct((t, heads * HEAD_DIM), F32)],
        [pltpu.VMEM((hp, tq, HEAD_DIM), F32), pltpu.VMEM((hp, tq, 2 * HEAD_DIM), F32)],
        ("parallel", "arbitrary"), (q, k, *([kv] * hp)))


def _flash_bwd_dq(q, k, kv, do, lse, delta, heads, name, job=None):
    t = q.shape[0]
    tq = _tile(t, 512)
    hp = _head_group(heads)
    nrep = tq // HEAD_DIM
    scale = 1.0 / math.sqrt(MLA_QK)

    def body(q_ref, k_ref, *rest):
        v_refs = rest[:hp]
        do_ref, lse_ref, d_ref, dq_ref = rest[hp:]
        i = pl.program_id(1)
        dq_ref[...] = jnp.zeros_like(dq_ref)

        def step(j, masked):
            rows = pl.ds(pl.multiple_of(j * tq, tq), tq)
            for hh in range(hp):
                hs = slice(hh * 256, (hh + 1) * 256)
                st = slice(hh * HEAD_DIM, (hh + 1) * HEAD_DIM)
                kk = k_ref[rows, hs]
                s = _dot(q_ref[:, hs], kk, _NT) * scale
                if masked:
                    s = _causal_mask(s, 0, 0)
                p = jnp.exp(s - jnp.tile(lse_ref[:, st], (1, nrep)))
                dp = _dot(do_ref[:, st], v_refs[hh][rows, :], _NT)
                ds = p * (dp - jnp.tile(d_ref[:, st], (1, nrep))) * scale
                dq_ref[:, hs] += _dot(ds, kk, _NN)

        def loop_body(j, carry):
            step(j, False)
            return carry

        lax.fori_loop(0, i, loop_body, 0)
        step(i, True)

    stat = pl.BlockSpec((tq, hp * HEAD_DIM), lambda h, i: (i, h))
    qs = pl.BlockSpec((tq, hp * 256), lambda h, i: (i, h))
    return _carry_call(
        job, body, name, (heads // hp, t // tq),
        [qs, pl.BlockSpec((t, hp * 256), lambda h, i: (0, h))] + _v_specs(t, hp, lambda i: 0) + [stat, stat, stat],
        [qs], [jax.ShapeDtypeStruct((t, heads * 256), F32)], [], ("parallel", "arbitrary"),
        (q, k, *([kv] * hp), do, lse, delta))


def _stat_rows(stat, heads):
    return jnp.transpose(stat[:, ::HEAD_DIM]).reshape(heads, 1, stat.shape[0])


def _flash_bwd_dkv(q, k, kv, do, lse_t, delta_t, heads, name, job=None):
    t = q.shape[0]
    tq = _tile(t, 512)
    nq = t // tq
    hp = _head_group(heads)
    scale = 1.0 / math.sqrt(MLA_QK)

    def body(q_ref, k_ref, *rest):
        v_refs = rest[:hp]
        do_ref, lse_ref, d_ref, dk_ref, dv_ref = rest[hp:]
        j = pl.program_id(1)
        dk_ref[...] = jnp.zeros_like(dk_ref)
        dv_ref[...] = jnp.zeros_like(dv_ref)

        def step(i, masked):
            rows = pl.ds(pl.multiple_of(i * tq, tq), tq)
            for hh in range(hp):
                hs = slice(hh * 256, (hh + 1) * 256)
                st = slice(hh * HEAD_DIM, (hh + 1) * HEAD_DIM)
                qq = q_ref[rows, hs]
                dd = do_ref[rows, st]
                s = _dot(k_ref[:, hs], qq, _NT) * scale
                if masked:
                    kj = lax.broadcasted_iota(jnp.int32, s.shape, 0)
                    qi = lax.broadcasted_iota(jnp.int32, s.shape, 1)
                    s = jnp.where(kj <= qi, s, NEG)
                p = jnp.exp(s - lse_ref[hh, :, rows])
                dv_ref[:, st] += _dot(p, dd, _NN)
                dp = _dot(v_refs[hh][...], dd, _NT)
                ds = p * (dp - d_ref[hh, :, rows]) * scale
                dk_ref[:, hs] += _dot(ds, qq, _NN)

        def loop_body(i, carry):
            step(i, False)
            return carry

        step(j, True)
        lax.fori_loop(j + 1, nq, loop_body, 0)

    ks = pl.BlockSpec((tq, hp * 256), lambda h, j: (j, h))
    row = pl.BlockSpec((hp, 1, t), lambda h, j: (h, 0, 0))
    return _carry_call(
        job, body, name, (heads // hp, nq),
        [pl.BlockSpec((t, hp * 256), lambda h, j: (0, h)), ks] + _v_specs(tq, hp, lambda j: j)
        + [pl.BlockSpec((t, hp * MLA_V), lambda h, j: (0, h)), row, row],
        [ks, pl.BlockSpec((tq, hp * MLA_V), lambda h, j: (j, h))],
        [jax.ShapeDtypeStruct((t, heads * 256), F32), jax.ShapeDtypeStruct((t, heads * MLA_V), F32)], [],
        ("parallel", "arbitrary"), (q, k, *([kv] * hp), do, lse_t, delta_t))


def _dil_view(a, d):
    t, w = a.shape
    return a.reshape(t // d, d * w)


def _dil_masks(first_block):
    qi = lax.broadcasted_iota(jnp.int32, (DIL_BLK, DIL_BLK), 0)
    kj = lax.broadcasted_iota(jnp.int32, (DIL_BLK, DIL_BLK), 1)
    return jnp.logical_and(kj >= qi, jnp.logical_not(first_block)), kj <= qi


def _dil_fwd(q, k, v, d, name):
    t, w = q.shape
    nb = t // d // DIL_BLK
    scale = 1.0 / math.sqrt(HEAD_DIM)

    def body(q_ref, kp_ref, kc_ref, vp_ref, vc_ref, o_ref, lse_ref):
        mask_p, mask_c = _dil_masks(pl.program_id(1) == 0)
        for hh in range(w // HEAD_DIM):
            sl = slice(hh * HEAD_DIM, (hh + 1) * HEAD_DIM)
            qh = q_ref[:, sl]
            sp = jnp.where(mask_p, _dot(qh, kp_ref[:, sl], _NT) * scale, NEG)
            sc = jnp.where(mask_c, _dot(qh, kc_ref[:, sl], _NT) * scale, NEG)
            m = jnp.maximum(jnp.max(sp, axis=1, keepdims=True), jnp.max(sc, axis=1, keepdims=True))
            pp = jnp.exp(sp - m)
            pc = jnp.exp(sc - m)
            l = jnp.sum(pp, axis=1, keepdims=True) + jnp.sum(pc, axis=1, keepdims=True)
            o_ref[:, sl] = (_dot(pp, vp_ref[:, sl], _NN) + _dot(pc, vc_ref[:, sl], _NN)) / l
            lse_ref[:, sl] = jnp.broadcast_to(m + jnp.log(l), (DIL_BLK, HEAD_DIM))

    cur = pl.BlockSpec((DIL_BLK, w), lambda r, n: (n, r))
    prev = pl.BlockSpec((DIL_BLK, w), lambda r, n: (jnp.maximum(n - 1, 0), r))
    shape = jax.ShapeDtypeStruct((t // d, d * w), F32)
    o, lse = pl.pallas_call(
        body, name=name, grid=(d, nb),
        in_specs=[cur, prev, cur, prev, cur], out_specs=[cur, cur], out_shape=[shape, shape],
        compiler_params=_params("parallel", "arbitrary"),
    )(_dil_view(q, d), _dil_view(k, d), _dil_view(k, d), _dil_view(v, d), _dil_view(v, d))
    return o.reshape(t, w), lse.reshape(t, w)


def _dil_combine(os_, lses, name):
    t, w = os_[0].shape
    tr = _tile(t, 256)
    n = len(os_)

    def body(*refs):
        o_ref, m_ref = refs[-2:]
        ls = [refs[n + p][...] for p in range(n)]
        m = functools.reduce(jnp.maximum, ls)
        es = [jnp.exp(l - m) for l in ls]
        den = functools.reduce(jnp.add, es)
        acc = es[0] / den * refs[0][...]
        for p in range(1, n):
            acc = acc + es[p] / den * refs[p][...]
        o_ref[...] = acc.astype(BF16)
        m_ref[...] = m + jnp.log(den)

    row = pl.BlockSpec((tr, w), lambda i: (i, 0))
    return pl.pallas_call(
        body, name=name, grid=(t // tr,), in_specs=[row] * (2 * n), out_specs=[row, row],
        out_shape=[jax.ShapeDtypeStruct((t, w), BF16), jax.ShapeDtypeStruct((t, w), F32)],
        compiler_params=_params("parallel"),
    )(*os_, *lses)


def _dil_bwd_dq(q, k, v, do, mt, delta, d, name):
    t, w = q.shape
    nb = t // d // DIL_BLK
    scale = 1.0 / math.sqrt(HEAD_DIM)

    def body(q_ref, kp_ref, kc_ref, vp_ref, vc_ref, do_ref, mt_ref, d_ref, dq_ref):
        mask_p, mask_c = _dil_masks(pl.program_id(1) == 0)
        for hh in range(w // HEAD_DIM):
            sl = slice(hh * HEAD_DIM, (hh + 1) * HEAD_DIM)
            qh = q_ref[:, sl]
            dd = do_ref[:, sl]
            mrow = mt_ref[:, hh * HEAD_DIM:hh * HEAD_DIM + 1]
            drow = d_ref[:, hh * HEAD_DIM:hh * HEAD_DIM + 1]
            sp = jnp.where(mask_p, _dot(qh, kp_ref[:, sl], _NT) * scale, NEG)
            sc = jnp.where(mask_c, _dot(qh, kc_ref[:, sl], _NT) * scale, NEG)
            dsp = jnp.exp(sp - mrow) * (_dot(dd, vp_ref[:, sl], _NT) - drow) * scale
            dsc = jnp.exp(sc - mrow) * (_dot(dd, vc_ref[:, sl], _NT) - drow) * scale
            dq_ref[:, sl] = _dot(dsp, kp_ref[:, sl], _NN) + _dot(dsc, kc_ref[:, sl], _NN)

    cur = pl.BlockSpec((DIL_BLK, w), lambda r, n: (n, r))
    prev = pl.BlockSpec((DIL_BLK, w), lambda r, n: (jnp.maximum(n - 1, 0), r))
    kv_, vv_ = _dil_view(k, d), _dil_view(v, d)
    dq = pl.pallas_call(
        body, name=name, grid=(d, nb),
        in_specs=[cur, prev, cur, prev, cur, cur, cur, cur], out_specs=cur,
        out_shape=jax.ShapeDtypeStruct((t // d, d * w), F32),
        compiler_params=_params("parallel", "arbitrary"),
    )(_dil_view(q, d), kv_, kv_, vv_, vv_, _dil_view(do, d), _dil_view(mt, d), _dil_view(delta, d))
    return dq.reshape(t, w)


def _dil_bwd_dkv(q, k, v, do, mt, delta, d, name):
    t, w = q.shape
    nb = t // d // DIL_BLK
    scale = 1.0 / math.sqrt(HEAD_DIM)

    def body(k_ref, v_ref, qs_ref, qn_ref, dos_ref, don_ref, ms_ref, mn_ref, ds_ref, dn_ref, dk_ref, dv_ref):
        last = pl.program_id(1) == nb - 1
        qi = lax.broadcasted_iota(jnp.int32, (DIL_BLK, DIL_BLK), 0)
        kj = lax.broadcasted_iota(jnp.int32, (DIL_BLK, DIL_BLK), 1)
        mask_s = kj <= qi
        mask_n = jnp.logical_and(kj >= qi, jnp.logical_not(last))
        for hh in range(w // HEAD_DIM):
            sl = slice(hh * HEAD_DIM, (hh + 1) * HEAD_DIM)
            st = slice(hh * HEAD_DIM, hh * HEAD_DIM + 1)
            kh, vh = k_ref[:, sl], v_ref[:, sl]
            dk = jnp.zeros((DIL_BLK, HEAD_DIM), F32)
            dv = jnp.zeros((DIL_BLK, HEAD_DIM), F32)
            for q_ref, do_ref, m_ref, d_ref, mask in ((qs_ref, dos_ref, ms_ref, ds_ref, mask_s),
                                                      (qn_ref, don_ref, mn_ref, dn_ref, mask_n)):
                qh, dd = q_ref[:, sl], do_ref[:, sl]
                s = jnp.where(mask, _dot(qh, kh, _NT) * scale, NEG)
                p = jnp.exp(s - m_ref[:, st])
                dv = dv + _dot(p, dd, _TN)
                dsv = p * (_dot(dd, vh, _NT) - d_ref[:, st]) * scale
                dk = dk + _dot(dsv, qh, _TN)
            dk_ref[:, sl] = dk
            dv_ref[:, sl] = dv

    cur = pl.BlockSpec((DIL_BLK, w), lambda r, n: (n, r))
    nxt = pl.BlockSpec((DIL_BLK, w), lambda r, n: (jnp.minimum(n + 1, nb - 1), r))
    shape = jax.ShapeDtypeStruct((t // d, d * w), F32)
    qv, dov, mv, dv_ = _dil_view(q, d), _dil_view(do, d), _dil_view(mt, d), _dil_view(delta, d)
    dk, dv = pl.pallas_call(
        body, name=name, grid=(d, nb),
        in_specs=[cur, cur, cur, nxt, cur, nxt, cur, nxt, cur, nxt], out_specs=[cur, cur], out_shape=[shape, shape],
        compiler_params=_params("parallel", "arbitrary"),
    )(_dil_view(k, d), _dil_view(v, d), qv, qv, dov, dov, mv, mv, dv_, dv_)
    return dk.reshape(t, w), dv.reshape(t, w)


def _adamw(w, g, m, v, name):
    r, c = w.shape
    tr = _tile(r, 256, 8)
    c1 = 1.0 - ADAM_B1 ** ADAM_STEP
    c2 = 1.0 - ADAM_B2 ** ADAM_STEP

    def body(w_ref, g_ref, m_ref, v_ref, d_ref, mo_ref, vo_ref):
        gg = g_ref[...]
        mn = ADAM_B1 * m_ref[...] + (1.0 - ADAM_B1) * gg
        vn = ADAM_B2 * v_ref[...] + (1.0 - ADAM_B2) * (gg * gg)
        mo_ref[...] = mn
        vo_ref[...] = vn
        d_ref[...] = -ADAM_LR * ((mn / c1) / (jnp.sqrt(vn / c2) + ADAM_EPS) + ADAM_WD * w_ref[...])

    row = pl.BlockSpec((tr, c), lambda i: (i, 0))
    shape = jax.ShapeDtypeStruct((r, c), F32)
    return pl.pallas_call(
        body, name=name, grid=(r // tr,), in_specs=[row] * 4, out_specs=[row] * 3, out_shape=[shape] * 3,
        compiler_params=_params("parallel"),
    )(w, g, m, v)


def _position():
    return lax.axis_index("x"), lax.axis_index("y"), lax.axis_index("c")


def _other_chips(x, y):
    return [(1 - x, y), (x, 1 - y), (1 - x, 1 - y)]


_ANY = pl.BlockSpec(memory_space=pl.ANY)


def _half_rows(a, half):
    return pl.ds(pl.multiple_of(half * (a // 2), 16), a // 2)


class _GatherJob:
    def __init__(self, shards):
        self.inputs = list(shards)
        n = len(shards)
        self.out_shapes = [jax.ShapeDtypeStruct((N_CHIPS,) + s.shape, s.dtype) for s in shards]
        self.scratch = [pltpu.SemaphoreType.DMA((2, 3, n)), pltpu.SemaphoreType.DMA((2, 3, n)),
                        pltpu.SemaphoreType.DMA((n,))]

    def _copies(self, w_refs, out_refs, sems):
        send_sems, recv_sems, local_sems = sems
        x, y, c = _position()
        me = 2 * x + y
        n = len(self.inputs)

        def copy(level, kk, i, src, dst, to):
            return functools.partial(
                pltpu.make_async_remote_copy, src_ref=src, dst_ref=dst, send_sem=send_sems.at[level, kk, i],
                recv_sem=recv_sems.at[level, kk, i], device_id=to, device_id_type=MESH)

        mine = [functools.partial(pltpu.make_async_copy, w_refs[i], out_refs[i].at[me], local_sems.at[i]) for i in range(n)]
        sends, landed, passed = [], [], []
        for i in range(n):
            a = self.inputs[i].shape[0]
            for kk, (px, py) in enumerate(_other_chips(x, y)):
                rows = _half_rows(a, c)
                sends.append(copy(0, kk, i, w_refs[i].at[rows], out_refs[i].at[me, rows], (px, py, c)))
                got = out_refs[i].at[2 * px + py, rows]
                landed.append((copy(0, kk, i, got, got, (px, py, c)), copy(1, kk, i, got, got, (x, y, 1 - c))))
                got = out_refs[i].at[2 * px + py, _half_rows(a, 1 - c)]
                passed.append(copy(1, kk, i, got, got, (x, y, 1 - c)))
        return mine, sends, landed, passed

    def start(self, w_refs, out_refs, sems):
        mine, sends, _, _ = self._copies(w_refs, out_refs, sems)
        for make in sends + mine:
            make().start()

    def finish(self, w_refs, out_refs, sems):
        mine, sends, landed, passed = self._copies(w_refs, out_refs, sems)
        forwards = []
        for arrival, forward in landed:
            arrival().wait_recv()
            forwards.append(forward())
            forwards[-1].start()
        for make in passed:
            make().wait_recv()
        for make in sends:
            make().wait_send()
        for cp in forwards:
            cp.wait_send()
        for make in mine:
            make().wait()


class _ExchangeJob:
    def __init__(self, sums):
        self.inputs = list(sums)
        n = len(sums)
        self.out_shapes = [jax.ShapeDtypeStruct((3,) + s.shape[1:], s.dtype) for s in sums]
        self.scratch = [pltpu.SemaphoreType.DMA((n, 3)), pltpu.SemaphoreType.DMA((n, 3))]

    def _copies(self, s_refs, out_refs, sems):
        send_sems, recv_sems = sems
        x, y, c = _position()
        return [pltpu.make_async_remote_copy(
            src_ref=s_refs[i].at[2 * px + py], dst_ref=out_refs[i].at[kk], send_sem=send_sems.at[i, kk],
            recv_sem=recv_sems.at[i, kk], device_id=(px, py, c), device_id_type=MESH)
            for i in range(len(self.inputs)) for kk, (px, py) in enumerate(_other_chips(x, y))]

    def start(self, s_refs, out_refs, sems):
        for cp in self._copies(s_refs, out_refs, sems):
            cp.start()

    def finish(self, s_refs, out_refs, sems):
        cps = self._copies(s_refs, out_refs, sems)
        for cp in cps:
            cp.wait_recv()
        for cp in cps:
            cp.wait_send()


def _run_job(job, name):
    ni, no = len(job.inputs), len(job.out_shapes)

    def body(*refs):
        job.start(refs[:ni], refs[ni:ni + no], refs[ni + no:])
        job.finish(refs[:ni], refs[ni:ni + no], refs[ni + no:])

    return pl.pallas_call(body, name=name, in_specs=[_ANY] * ni, out_specs=[_ANY] * no, out_shape=job.out_shapes,
                          scratch_shapes=job.scratch)(*job.inputs)


def _carry(job, body, n_in, n_out, grid):
    if job is None:
        return body
    ni, no, ns = len(job.inputs), len(job.out_shapes), len(job.scratch)

    def carried(*refs):
        own_in, job_in = refs[:n_in], refs[n_in:n_in + ni]
        own_out = refs[n_in + ni:n_in + ni + n_out]
        job_out = refs[n_in + ni + n_out:n_in + ni + n_out + no]
        scratch = refs[n_in + ni + n_out + no:]
        own_scratch, job_scratch = scratch[:len(scratch) - ns], scratch[len(scratch) - ns:]
        ids = [pl.program_id(ax) for ax in range(len(grid))]
        first = functools.reduce(jnp.logical_and, [i == 0 for i in ids])
        last = functools.reduce(jnp.logical_and, [i == g - 1 for i, g in zip(ids, grid)])

        @pl.when(first)
        def _():
            job.start(job_in, job_out, job_scratch)

        body(*own_in, *own_out, *own_scratch)

        @pl.when(last)
        def _():
            job.finish(job_in, job_out, job_scratch)

    return carried


def _carry_call(job, body, name, grid, in_specs, out_specs, out_shape, scratch_shapes, sem, args):
    n_out = len(out_shape)
    if job is None:
        outs = pl.pallas_call(body, name=name, grid=grid, in_specs=in_specs, out_specs=out_specs, out_shape=out_shape,
                              scratch_shapes=scratch_shapes, compiler_params=_params(*sem))(*args)
        return list(outs), None
    outs = pl.pallas_call(
        _carry(job, body, len(in_specs), n_out, grid), name=name, grid=grid,
        in_specs=list(in_specs) + [_ANY] * len(job.inputs), out_specs=list(out_specs) + [_ANY] * len(job.out_shapes),
        out_shape=list(out_shape) + job.out_shapes, scratch_shapes=list(scratch_shapes) + job.scratch,
        compiler_params=_params(*(["arbitrary"] * len(grid))),
    )(*args, *job.inputs)
    return list(outs[:n_out]), list(outs[n_out:])


def _swap_halves(gs, name):
    n = len(gs)

    def body(*refs):
        g_refs, out_refs = refs[:n], refs[n:2 * n]
        send_sems, recv_sems = refs[2 * n:]
        x, y, c = _position()
        cps = []
        for i in range(n):
            cp = pltpu.make_async_remote_copy(
                src_ref=g_refs[i].at[:, _half_rows(gs[i].shape[1], 1 - c)], dst_ref=out_refs[i],
                send_sem=send_sems.at[i], recv_sem=recv_sems.at[i], device_id=(x, y, 1 - c), device_id_type=MESH)
            cp.start()
            cps.append(cp)
        for cp in cps:
            cp.wait_recv()
        for cp in cps:
            cp.wait_send()

    return pl.pallas_call(
        body, name=name, in_specs=[_ANY] * n, out_specs=[_ANY] * n,
        out_shape=[jax.ShapeDtypeStruct((g.shape[0], g.shape[1] // 2, g.shape[2]), g.dtype) for g in gs],
        scratch_shapes=[pltpu.SemaphoreType.DMA((n,)), pltpu.SemaphoreType.DMA((n,))],
    )(*gs)


def _share_halves(bufs, name):
    n = len(bufs)

    def body(*refs):
        in_refs, out_refs = refs[:n], refs[n:2 * n]
        send_sems, recv_sems = refs[2 * n:]
        x, y, c = _position()
        cps = []
        for i in range(n):
            cp = pltpu.make_async_remote_copy(
                src_ref=in_refs[i].at[c], dst_ref=out_refs[i].at[c], send_sem=send_sems.at[i], recv_sem=recv_sems.at[i],
                device_id=(x, y, 1 - c), device_id_type=MESH)
            cp.start()
            cps.append(cp)
        for i in range(n):
            pltpu.make_async_remote_copy(
                src_ref=in_refs[i].at[1 - c], dst_ref=out_refs[i].at[1 - c], send_sem=send_sems.at[i],
                recv_sem=recv_sems.at[i], device_id=(x, y, 1 - c), device_id_type=MESH).wait_recv()
        for cp in cps:
            cp.wait_send()

    return pl.pallas_call(
        body, name=name, in_specs=[_ANY] * n, out_specs=[_ANY] * n,
        out_shape=[jax.ShapeDtypeStruct(b.shape, b.dtype) for b in bufs],
        input_output_aliases={i: i for i in range(n)},
        scratch_shapes=[pltpu.SemaphoreType.DMA((n,)), pltpu.SemaphoreType.DMA((n,))],
    )(*bufs)


def _add_pair(g, got, name):
    n, a, b = g.shape
    ah = a // 2
    tr = _tile(ah, 256, 16)

    def body(c_ref, a_ref, b_ref, o_ref):
        o_ref[...] = (a_ref[...].astype(F32) + b_ref[...].astype(F32)).astype(BF16)

    blk = pl.BlockSpec((None, tr, b), lambda j, i, c_ref: (j, i, 0))
    return pl.pallas_call(
        body, name=name,
        grid_spec=pltpu.PrefetchScalarGridSpec(
            num_scalar_prefetch=1, grid=(n, ah // tr),
            in_specs=[pl.BlockSpec((None, None, tr, b), lambda j, i, c_ref: (j, c_ref[0], i, 0)), blk], out_specs=blk),
        out_shape=jax.ShapeDtypeStruct((n, ah, b), BF16),
        compiler_params=_params("parallel", "parallel"),
    )(lax.axis_index("c").reshape(1).astype(jnp.int32), g.reshape(n, 2, ah, b), got)


def _add_chips(sums, got, name):
    n, ah, b = sums.shape
    tr = _tile(ah, 256, 16)

    def body(at_ref, a_ref, b_ref, o_ref):
        o_ref[...] = ((a_ref[...].astype(F32) + b_ref[0].astype(F32)) + b_ref[1].astype(F32)) + b_ref[2].astype(F32)

    at = jnp.stack([2 * lax.axis_index("x") + lax.axis_index("y"), lax.axis_index("c")]).astype(jnp.int32)
    return pl.pallas_call(
        body, name=name,
        grid_spec=pltpu.PrefetchScalarGridSpec(
            num_scalar_prefetch=1, grid=(ah // tr,),
            in_specs=[pl.BlockSpec((None, tr, b), lambda i, at_ref: (at_ref[0], i, 0)),
                      pl.BlockSpec((3, tr, b), lambda i, at_ref: (0, i, 0))],
            out_specs=pl.BlockSpec((None, tr, b), lambda i, at_ref: (at_ref[1], i, 0))),
        out_shape=jax.ShapeDtypeStruct((2, ah, b), F32),
        compiler_params=_params("parallel"),
    )(at, sums, got)


def _allgather_small(v, name):
    m_per, w = v.shape

    def body(x_ref, out_ref, send_sems, recv_sems, local_sem):
        x, y, c = _position()
        me, sibling = (x, y, c), (x, y, 1 - c)
        chips = _other_chips(x, y)

        def rows(px, py, pc):
            return out_ref.at[pl.ds(pl.multiple_of((4 * px + 2 * py + pc) * m_per, 8), m_per), :]

        def copy(kk, block, to, src=None):
            return pltpu.make_async_remote_copy(
                src_ref=rows(*block) if src is None else src, dst_ref=rows(*block), send_sem=send_sems.at[kk],
                recv_sem=recv_sems.at[kk], device_id=to, device_id_type=MESH)

        mine = pltpu.make_async_copy(x_ref, rows(*me), local_sem)
        mine.start()
        first = [copy(0, me, sibling, src=x_ref)]
        first += [copy(1 + j, me, (*chip, c), src=x_ref) for j, chip in enumerate(chips)]
        for cp in first:
            cp.start()
        passed = [copy(4 + j, (*chip, c), sibling) for j, chip in enumerate(chips)]
        for j, chip in enumerate(chips):
            copy(1 + j, (*chip, c), me).wait_recv()
            passed[j].start()
        copy(0, sibling, me).wait_recv()
        for j, chip in enumerate(chips):
            copy(4 + j, (*chip, 1 - c), me).wait_recv()
        for cp in first + passed:
            cp.wait_send()
        mine.wait()

    return pl.pallas_call(
        body, name=name,
        out_shape=jax.ShapeDtypeStruct((N_DEV * m_per, w), v.dtype),
        in_specs=[pl.BlockSpec(memory_space=pltpu.VMEM)], out_specs=pl.BlockSpec(memory_space=pltpu.VMEM),
        scratch_shapes=[pltpu.SemaphoreType.DMA((7,)), pltpu.SemaphoreType.DMA((7,)), pltpu.SemaphoreType.DMA],
    )(v)


def _sum_blocks(a, n, name):
    m_per, w = a.shape[0] // n, a.shape[1]

    def body(a_ref, o_ref):
        s = a_ref[0:m_per, :]
        for j in range(1, n):
            s = s + a_ref[j * m_per:(j + 1) * m_per, :]
        o_ref[...] = s

    return pl.pallas_call(body, name=name, out_shape=jax.ShapeDtypeStruct((m_per, w), a.dtype))(a)


def _pack_gains(d):
    flat = jnp.concatenate([d[n].reshape(1, -1) for n in GAIN_NAMES], axis=1)
    n = flat.shape[-1]
    rows = -(-n // (PACK_W * 8)) * 8
    return jnp.pad(flat, ((0, 0), (0, rows * PACK_W - n))).reshape(rows, PACK_W)


def _to_slabs(g):
    a, nb = g.shape
    return jnp.transpose(g.reshape(a, N_CHIPS, nb // N_CHIPS), (1, 0, 2))


def _from_slabs(s):
    n, a, b = s.shape
    return jnp.transpose(s, (1, 0, 2)).reshape(a, n * b)


def _unpack_gains(slab, shapes):
    flat = slab.reshape(-1)
    out, off = {}, 0
    for n in GAIN_NAMES:
        a, b = shapes[n]
        out[n] = flat[off:off + a * b].reshape(a, b)
        off += a * b
    return out


def _reduce_begin(gs):
    got = _swap_halves(gs, "rs_swap_halves")
    return _ExchangeJob([_add_pair(g, h, "rs_add_pair") for g, h in zip(gs, got)])


def _reduce_end(job, got):
    halves = [_add_chips(s, h, "rs_add_chips") for s, h in zip(job.inputs, got)]
    return [b.reshape(b.shape[1] * 2, b.shape[2]) for b in _share_halves(halves, "rs_share_halves")]


def _ffn_forward(h, wl, pre, tag, job=None):
    (g, u, a), got = _ffn_up(h, wl[pre + '_w_gate'], wl[pre + '_w_up'], "ffn_up" + tag, job)
    wd = wl[pre + '_w_down']
    z = _mm([(a, wd.reshape(-1, wd.shape[2]))], 'nn', F32, "ffn_down" + tag, tn=1024, tk=wd.shape[1])
    return z, (g, u, a), got


def _ffn_backward(dz, h, saved, wl, pre, grads, tag):
    g, u, a = saved
    wd = wl[pre + '_w_down']
    dg, du = _ffn_bwd_act(dz, wd.reshape(-1, wd.shape[2]), g, u, "ffn_bwd_act" + tag)
    grads[pre + '_w_down'] = _mm([(a, dz)], 'tn', BF16, "ffn_dw_down" + tag, tm=wd.shape[1], tn=1024, tk=1024).reshape(wd.shape)
    grads[pre + '_w_gate'] = _mm([(h, dg)], 'tn', BF16, "ffn_dw_gate" + tag, tk=1024, out_slabs=N_CHIPS)
    grads[pre + '_w_up'] = _mm([(h, du)], 'tn', BF16, "ffn_dw_up" + tag, tk=1024, out_slabs=N_CHIPS)
    return _mm([(dg, wl[pre + '_w_gate']), (du, wl[pre + '_w_up'])], 'nt', F32, "ffn_dh" + tag, tn=1024, b_slabs=True)


def kernel(x, positions, ffn1_pre_g, ffn1_post_g, ffn1_w_gate, ffn1_w_up, ffn1_w_down, mix_pre_g, mix_post_g, w_in, mla_q_norm_g, mla_w_uq, mla_kv_norm_g, mla_w_ukv, w_o, ffn2_pre_g, ffn2_post_g, ffn2_w_gate, ffn2_w_up, ffn2_w_down, loss_target, m_ffn1_pre_g, m_ffn1_post_g, m_ffn1_w_gate, m_ffn1_w_up, m_ffn1_w_down, m_mix_pre_g, m_mix_post_g, m_w_in, m_mla_q_norm_g, m_mla_w_uq, m_mla_kv_norm_g, m_mla_w_ukv, m_w_o, m_ffn2_pre_g, m_ffn2_post_g, m_ffn2_w_gate, m_ffn2_w_up, m_ffn2_w_down, v_ffn1_pre_g, v_ffn1_post_g, v_ffn1_w_gate, v_ffn1_w_up, v_ffn1_w_down, v_mix_pre_g, v_mix_post_g, v_w_in, v_mla_q_norm_g, v_mla_w_uq, v_mla_kv_norm_g, v_mla_w_ukv, v_w_o, v_ffn2_pre_g, v_ffn2_post_g, v_ffn2_w_gate, v_ffn2_w_up, v_ffn2_w_down):
    given = dict(locals())
    ws = {n: given[n] for n in WEIGHT_NAMES}
    ms = {n: given['m_' + n] for n in WEIGHT_NAMES}
    vs = {n: given['v_' + n] for n in WEIGHT_NAMES}

    depth = ffn1_pre_g.shape[0]
    t, d_model = x.shape[1], x.shape[2]
    rq, rkv = mla_q_norm_g.shape[1], mla_kv_norm_g.shape[1]
    h_mla = mla_w_uq.shape[2] * N_CHIPS // MLA_QK
    wd_ = (w_in.shape[2] * N_CHIPS - rq - rkv - MLA_ROPE) // 3
    assert rq == rkv and (rq + rkv) % wd_ == 0 and wd_ % HEAD_DIM == 0
    assert mla_w_ukv.shape[2] * N_CHIPS == h_mla * 256 and w_o.shape[1] * N_CHIPS == h_mla * MLA_V + wd_
    lat = rq + rkv
    in_p = lat + 3 * wd_ + LANES
    qd_block = lat // wd_
    kr_block = (lat + 3 * wd_) // LANES
    gain_shapes = {n: ws[n].shape for n in GAIN_NAMES}

    pos = positions[0].astype(F32)[:, None]

    def cos_sin(dim):
        inv = ROPE_THETA ** (-jnp.arange(0, dim, 2, dtype=F32) / dim)
        return jnp.cos(pos * inv), jnp.sin(pos * inv)

    cos_a, sin_a = cos_sin(MLA_ROPE)
    cos_p, sin_p = cos_sin(PARTIAL_ROPE)
    tab_q = [_rot_tables(cos_a, sin_a, MLA_QK_PAD, MLA_NOPE, f) for f in (True, False)]
    tab_kr = [_rot_tables(cos_a, sin_a, LANES, 0, f) for f in (True, False)]
    tab_d = [_rot_tables(cos_p, sin_p, HEAD_DIM, 0, f) for f in (True, False)]
    s_a, s_p = MLA_ROPE // 2, PARTIAL_ROPE // 2

    w16 = {n: ws[n].astype(BF16) for n in MATRIX_NAMES}
    parts = {'ffn1': [n for n in MATRIX_NAMES if n.startswith('ffn1')], 'ffn2': [n for n in MATRIX_NAMES if n.startswith('ffn2')],
             'mix': [n for n in MATRIX_NAMES if not n.startswith('ffn')]}

    def gather_job(l, part):
        return _GatherJob([w16[n][l] for n in parts[part]]) if l < depth else None

    layers = [{} for _ in range(depth)]
    layers[0].update(zip(parts['ffn1'], _run_job(gather_job(0, 'ffn1'), "allgather_weights")))

    xc = x[0]
    h = _rms_fwd(xc, ffn1_pre_g[0], "rms_first")
    saved = []
    dy = loss = None
    for l in range(depth):
        wl, sv = layers[l], {}
        sv['x0'], sv['h1'] = xc, h
        z, sv['ffn1'], got = _ffn_forward(h, wl, 'ffn1', "_1", gather_job(l, 'mix'))
        wl.update(zip(parts['mix'], got))
        w_in_l = _from_slabs(wl['w_in'])
        wl['w_in'] = jnp.concatenate([w_in_l[:, :lat], w_in_l[:, lat + MLA_ROPE:], w_in_l[:, lat:lat + MLA_ROPE],
                                      jnp.zeros((d_model, LANES - MLA_ROPE), BF16)], axis=1)
        wl['mla_w_uq'] = jnp.pad(_from_slabs(wl['mla_w_uq']).reshape(rq, h_mla, MLA_QK),
                                 ((0, 0), (0, 0), (0, MLA_QK_PAD - MLA_QK))).reshape(rq, h_mla * MLA_QK_PAD)
        wl['w_o'] = wl['w_o'].reshape(-1, d_model)
        sv['z1'] = z
        xc, h = _post_pre(xc, z, ffn1_post_g[l], 0.5, mix_pre_g[l], "post_pre_1")
        sv['x1'], sv['h2'] = xc, h

        proj = _mm([(h, wl['w_in'])], 'nn', BF16, "mix_in", tn=1408, tk=d_model)
        cqn = _rms_fwd(proj, mla_q_norm_g[l], "rms_cq", col_block=0)
        ckvn = _rms_fwd(proj, mla_kv_norm_g[l], "rms_ckv", col_block=1)
        qf = _rot([_mm([(cqn, wl['mla_w_uq'])], 'nn', F32, "mla_uq")], tab_q[0], s_a, "rope_q")
        kv = _mm([(ckvn, wl['mla_w_ukv'])], 'nn', BF16, "mla_ukv", b_slabs=True)
        kf = _mla_kfull(kv, proj, kr_block, tab_kr[0], h_mla, "mla_kfull")
        (oa, lse), got = _flash_fwd(qf, kf, kv, h_mla, "mla_attn", gather_job(l, 'ffn2'))
        wl.update(zip(parts['ffn2'], got))
        qd = _rot([proj], tab_d[0], s_p, "rope_qd", col_block=qd_block, width=wd_)
        kd = _rot([proj], tab_d[0], s_p, "rope_kd", col_block=qd_block + 1, width=wd_)
        vd = proj[:, lat + 2 * wd_:lat + 3 * wd_]
        outs = [_dil_fwd(qd, kd, vd, dil, "dil_attn_%d" % dil) for _, dil in DIL_PATTERNS]
        ob, mt = _dil_combine([o for o, _ in outs], [s for _, s in outs], "dil_combine")
        o = jnp.concatenate([oa, ob], axis=1)
        z = _mm([(o, wl['w_o'])], 'nn', F32, "mix_out", tk=o.shape[1])
        sv.update(proj=proj, cqn=cqn, ckvn=ckvn, qf=qf, kv=kv, kf=kf, oa=oa, lse=lse, qd=qd, kd=kd, vd=vd, ob=ob,
                  mt=mt, o=o, z2=z)
        xc, h = _post_pre(xc, z, mix_post_g[l], 1.0, ffn2_pre_g[l], "post_pre_2")
        sv['x2'], sv['h3'] = xc, h

        z, sv['ffn2'], got = _ffn_forward(h, wl, 'ffn2', "_2", gather_job(l + 1, 'ffn1'))
        sv['z3'] = z
        if got is not None:
            layers[l + 1].update(zip(parts['ffn1'], got))
        if l + 1 < depth:
            xc, h = _post_pre(xc, z, ffn2_post_g[l], 0.5, ffn1_pre_g[l + 1], "post_pre_3")
        else:
            dy, loss = _post_loss(xc, z, ffn2_post_g[l], 0.5, loss_target[0], "post_loss")
        saved.append(sv)

    dx = dy
    gain_grads = {n: [None] * depth for n in GAIN_NAMES}
    shard_grads = [{} for _ in range(depth)]
    late_names = parts['mix'] + parts['ffn1']
    late = None
    for l in reversed(range(depth)):
        wl, sv, grads = layers[l], saved[l], {}

        dz, gain_grads['ffn2_post_g'][l] = _rms_bwd(sv['z3'], ffn2_post_g[l], dx, 0.5, "rms_bwd_post_f", out_dtype=BF16)
        dh = _ffn_backward(dz, sv['h3'], sv['ffn2'], wl, 'ffn2', grads, "_2")
        dx, gain_grads['ffn2_pre_g'][l] = _rms_bwd(sv['x2'], ffn2_pre_g[l], dh, 1.0, "rms_bwd_pre", res=dx)
        early = _reduce_begin([grads[n] for n in parts['ffn2']])

        dz, gain_grads['mix_post_g'][l] = _rms_bwd(sv['z2'], mix_post_g[l], dx, 1.0, "rms_bwd_post_m", out_dtype=BF16)
        do = _mm([(dz, wl['w_o'])], 'nt', BF16, "mix_out_dx", tk=d_model)
        grads['w_o'] = _mm([(sv['o'], dz)], 'tn', BF16, "mix_out_dw", tn=1024, tk=1024).reshape(N_CHIPS, -1, d_model)
        do_a, do_b = do[:, :h_mla * MLA_V], do[:, h_mla * MLA_V:]

        delta = _delta(do_a, sv['oa'], "delta_a")
        (dqf,), got = _flash_bwd_dq(sv['qf'], sv['kf'], sv['kv'], do_a, sv['lse'], delta, h_mla, "mla_attn_dq", early)
        shard_grads[l].update(zip(parts['ffn2'], _reduce_end(early, got)))
        (dkf, dva), got = _flash_bwd_dkv(sv['qf'], sv['kf'], sv['kv'], do_a, _stat_rows(sv['lse'], h_mla),
                                         _stat_rows(delta, h_mla), h_mla, "mla_attn_dkv", late[1] if late else None)
        if late:
            shard_grads[late[0]].update(zip(late_names, _reduce_end(late[1], got)))
        dq = _rot([dqf], tab_q[1], s_a, "rope_q_bwd")
        grads['mla_w_uq'] = _mm([(sv['cqn'], dq)], 'tn', BF16, "mla_uq_dw", tk=1024)
        dcqn = _mm([(dq, wl['mla_w_uq'])], 'nt', F32, "mla_uq_dx", tk=1024)
        dcq, gain_grads['mla_q_norm_g'][l] = _rms_bwd(sv['proj'], mla_q_norm_g[l], dcqn, 1.0, "rms_bwd_cq",
                                                      col_block=0, out_dtype=BF16)
        dkv, dkr = _mla_dkv(dkf, dva, tab_kr[1], h_mla, "mla_dkv")
        grads['mla_w_ukv'] = _mm([(sv['ckvn'], dkv)], 'tn', BF16, "mla_ukv_dw", tk=1024, out_slabs=N_CHIPS)
        dckvn = _mm([(dkv, wl['mla_w_ukv'])], 'nt', F32, "mla_ukv_dx", b_slabs=True)
        dckv, gain_grads['mla_kv_norm_g'][l] = _rms_bwd(sv['proj'], mla_kv_norm_g[l], dckvn, 1.0, "rms_bwd_ckv",
                                                        col_block=1, out_dtype=BF16)

        delta = _delta(do_b, sv['ob'], "delta_b")
        dqs, dks, dvs = [], [], []
        for _, dil in DIL_PATTERNS:
            dqs.append(_dil_bwd_dq(sv['qd'], sv['kd'], sv['vd'], do_b, sv['mt'], delta, dil, "dil_attn_dq_%d" % dil))
            dk_, dv_ = _dil_bwd_dkv(sv['qd'], sv['kd'], sv['vd'], do_b, sv['mt'], delta, dil, "dil_attn_dkv_%d" % dil)
            dks.append(dk_)
            dvs.append(dv_)
        dqd = _rot(dqs, tab_d[1], s_p, "rope_qd_bwd")
        dkd = _rot(dks, tab_d[1], s_p, "rope_kd_bwd")
        dvd = _rot(dvs, None, 0, "sum_dvd")
        dproj = jnp.concatenate([dcq, dckv, dqd, dkd, dvd, dkr], axis=1)
        grads['w_in'] = _mm([(sv['h2'], dproj)], 'tn', BF16, "mix_in_dw", tn=1408, tk=1024)
        dh = _mm([(dproj, wl['w_in'])], 'nt', F32, "mix_in_dx", tn=1024, tk=1408)
        dx, gain_grads['mix_pre_g'][l] = _rms_bwd(sv['x1'], mix_pre_g[l], dh, 1.0, "rms_bwd_pre", res=dx)

        dz, gain_grads['ffn1_post_g'][l] = _rms_bwd(sv['z1'], ffn1_post_g[l], dx, 0.5, "rms_bwd_post_f", out_dtype=BF16)
        dh = _ffn_backward(dz, sv['h1'], sv['ffn1'], wl, 'ffn1', grads, "_1")
        dx, gain_grads['ffn1_pre_g'][l] = _rms_bwd(sv['x0'], ffn1_pre_g[l], dh, 1.0, "rms_bwd_pre", res=dx)

        gp = grads['w_in']
        grads['w_in'] = _to_slabs(jnp.concatenate(
            [gp[:, :lat], gp[:, in_p - LANES:in_p - LANES + MLA_ROPE], gp[:, lat:lat + 3 * wd_]], axis=1))
        grads['mla_w_uq'] = _to_slabs(
            grads['mla_w_uq'].reshape(rq, h_mla, MLA_QK_PAD)[:, :, :MLA_QK].reshape(rq, h_mla * MLA_QK))
        late = (l, _reduce_begin([grads[n] for n in late_names]))
    shard_grads[0].update(zip(late_names, _reduce_end(late[1], _run_job(late[1], "rs_exchange_chips"))))

    gg = _pack_gains({n: jnp.stack(gain_grads[n]) for n in GAIN_NAMES})
    gg = _sum_blocks(_allgather_small(gg, "allgather_gain_grads"), N_DEV, "sum_gain_grads")

    grad_w, delta_w, new_m, new_v = {}, {}, {}, {}
    dg, mg, vg = _adamw(_pack_gains(ws), gg, _pack_gains(ms), _pack_gains(vs), "adamw_gains")
    for dst, slab in ((grad_w, gg), (delta_w, dg), (new_m, mg), (new_v, vg)):
        dst.update(_unpack_gains(slab, gain_shapes))
    for n in MATRIX_NAMES:
        shape = ws[n].shape
        g = jnp.stack([shard_grads[l][n] for l in range(depth)])
        flat = lambda a: a.reshape(shape[0] * shape[1], shape[2])
        d_, m_, v_ = _adamw(flat(ws[n]), flat(g), flat(ms[n]), flat(vs[n]), "adamw_" + n)
        grad_w[n], delta_w[n], new_m[n], new_v[n] = g, d_.reshape(shape), m_.reshape(shape), v_.reshape(shape)

    loss = lax.psum(loss[0, 0], ("x", "y", "c"))
    return (loss, dx[None], *[grad_w[n] for n in WEIGHT_NAMES], *[delta_w[n] for n in WEIGHT_NAMES],
            *[new_m[n] for n in WEIGHT_NAMES], *[new_v[n] for n in WEIGHT_NAMES])
```

```python
import functools
import math

import jax
import jax.numpy as jnp
from jax import lax
from jax.experimental import pallas as pl
from jax.experimental.pallas import tpu as pltpu

F32 = jnp.float32
BF16 = jnp.bfloat16

HEAD_DIM = 128
MLA_NOPE = 128
MLA_ROPE = 64
MLA_V = 128
MLA_QK = MLA_NOPE + MLA_ROPE
MLA_QK_PAD = 256
DIL_PATTERNS = ((128, 1), (512, 4), (2048, 16))
DIL_BLK = 128
PARTIAL_ROPE = HEAD_DIM // 4
ROPE_THETA = 500000.0
RMS_EPS = 1e-6
NEG = -1e30
ADAM_LR, ADAM_B1, ADAM_B2, ADAM_EPS, ADAM_WD, ADAM_STEP = 0.001, 0.9, 0.999, 1e-08, 0.01, 10

N_CHIPS = 4
N_DEV = 8
LANES = 128
PACK_W = 1024
VMEM_LIMIT = 56 * 1024 * 1024
MESH = pl.DeviceIdType.MESH

WEIGHT_NAMES = ('ffn1_pre_g', 'ffn1_post_g', 'ffn1_w_gate', 'ffn1_w_up', 'ffn1_w_down', 'mix_pre_g', 'mix_post_g',
                'w_in', 'mla_q_norm_g', 'mla_w_uq', 'mla_kv_norm_g', 'mla_w_ukv', 'w_o', 'ffn2_pre_g', 'ffn2_post_g',
                'ffn2_w_gate', 'ffn2_w_up', 'ffn2_w_down')
GAIN_NAMES = tuple(n for n in WEIGHT_NAMES if n.endswith('_g'))
MATRIX_NAMES = tuple(n for n in WEIGHT_NAMES if not n.endswith('_g'))


def _tile(dim, pref, mult=LANES):
    if dim <= pref:
        return dim
    t = (pref // mult) * mult
    while t >= mult:
        if dim % t == 0:
            return t
        t -= mult
    return dim


def _params(*sem):
    return pltpu.CompilerParams(dimension_semantics=sem, vmem_limit_bytes=VMEM_LIMIT)


def _dot(a, b, dims):
    return lax.dot_general(a.astype(BF16), b.astype(BF16), (dims, ((), ())), preferred_element_type=F32)


_NN = ((1,), (0,))
_NT = ((1,), (1,))
_TN = ((0,), (0,))


def _mm(pairs, mode, out_dtype, name, tm=1024, tn=512, tk=512, b_slabs=False, out_slabs=1):
    a0, b0 = pairs[0]
    if mode == 'nn':
        m, k = a0.shape
        n = b0.shape[0] * b0.shape[2] if b_slabs else b0.shape[1]
        tn = b0.shape[2] if b_slabs else tn
    elif mode == 'nt':
        m, k = a0.shape
        n = b0.shape[1] if b_slabs else b0.shape[0]
        tk = b0.shape[2] if b_slabs else tk
    else:
        (k, m), n = a0.shape, b0.shape[1]
        tn = n // out_slabs if out_slabs > 1 else tn
    tm, tn, tk = _tile(m, tm), _tile(n, tn), _tile(k, tk)
    nk = k // tk
    dims = {'nn': _NN, 'nt': _NT, 'tn': _TN}[mode]
    if mode == 'tn':
        a_spec = pl.BlockSpec((tk, tm), lambda i, j, kk: (kk, i))
    else:
        a_spec = pl.BlockSpec((tm, tk), lambda i, j, kk: (i, kk))
    if mode == 'nt':
        b_spec = (pl.BlockSpec((None, tn, tk), lambda i, j, kk: (kk, j, 0)) if b_slabs
                  else pl.BlockSpec((tn, tk), lambda i, j, kk: (j, kk)))
    else:
        b_spec = (pl.BlockSpec((None, tk, tn), lambda i, j, kk: (j, kk, 0)) if b_slabs
                  else pl.BlockSpec((tk, tn), lambda i, j, kk: (kk, j)))
    if out_slabs > 1:
        out_spec = pl.BlockSpec((None, tm, tn), lambda i, j, kk: (j, i, 0))
        out_shape = jax.ShapeDtypeStruct((out_slabs, m, tn), out_dtype)
    else:
        out_spec = pl.BlockSpec((tm, tn), lambda i, j, kk: (i, j))
        out_shape = jax.ShapeDtypeStruct((m, n), out_dtype)
    n_pairs = len(pairs)

    def body(*refs):
        o_ref = refs[2 * n_pairs]

        def partial_sum():
            s = _dot(refs[0][...], refs[1][...], dims)
            for p in range(1, n_pairs):
                s = s + _dot(refs[2 * p][...], refs[2 * p + 1][...], dims)
            return s

        if nk == 1:
            o_ref[...] = partial_sum().astype(out_dtype)
        else:
            acc = refs[2 * n_pairs + 1]
            kk = pl.program_id(2)

            @pl.when(kk == 0)
            def _():
                acc[...] = jnp.zeros_like(acc)

            acc[...] += partial_sum()

            @pl.when(kk == nk - 1)
            def _():
                o_ref[...] = acc[...].astype(out_dtype)

    return pl.pallas_call(
        body, name=name, grid=(m // tm, n // tn, nk),
        in_specs=[a_spec, b_spec] * n_pairs,
        out_specs=out_spec, out_shape=out_shape,
        scratch_shapes=[pltpu.VMEM((tm, tn), F32)] if nk > 1 else [],
        compiler_params=_params("parallel", "parallel", "arbitrary"),
    )(*[t for pair in pairs for t in pair])


def _ffn_up(h, wg, wu, name, job=None):
    m, d = h.shape
    n_slabs, _, fs = wg.shape
    tm = _tile(m, 512)

    def body(h_ref, wg_ref, wu_ref, g_ref, u_ref, a_ref):
        g = _dot(h_ref[...], wg_ref[...], _NN)
        u = _dot(h_ref[...], wu_ref[...], _NN)
        g_ref[...] = g.astype(BF16)
        u_ref[...] = u.astype(BF16)
        a_ref[...] = (g * jax.nn.sigmoid(g) * u).astype(BF16)

    w_spec = pl.BlockSpec((None, d, fs), lambda j, i: (j, 0, 0))
    o_spec = pl.BlockSpec((tm, fs), lambda j, i: (i, j))
    shape = jax.ShapeDtypeStruct((m, n_slabs * fs), BF16)
    return _carry_call(job, body, name, (n_slabs, m // tm),
                       [pl.BlockSpec((tm, d), lambda j, i: (i, 0)), w_spec, w_spec], [o_spec] * 3, [shape] * 3, [],
                       ("parallel", "arbitrary"), (h, wg, wu))


def _ffn_bwd_act(dz, wd, g, u, name):
    m, d = dz.shape
    f = wd.shape[0]
    tm, tn = _tile(m, 1024), _tile(f, 512)

    def body(dz_ref, wd_ref, g_ref, u_ref, dg_ref, du_ref):
        da = _dot(dz_ref[...], wd_ref[...], _NT)
        gg = g_ref[...].astype(F32)
        uu = u_ref[...].astype(F32)
        sg = jax.nn.sigmoid(gg)
        du_ref[...] = (da * (gg * sg)).astype(BF16)
        dg_ref[...] = (da * uu * (sg * (1.0 + gg * (1.0 - sg)))).astype(BF16)

    t_spec = pl.BlockSpec((tm, tn), lambda i, j: (i, j))
    shape = jax.ShapeDtypeStruct((m, f), BF16)
    return pl.pallas_call(
        body, name=name, grid=(m // tm, f // tn),
        in_specs=[pl.BlockSpec((tm, d), lambda i, j: (i, 0)), pl.BlockSpec((tn, d), lambda i, j: (j, 0)), t_spec, t_spec],
        out_specs=[t_spec] * 2, out_shape=[shape] * 2,
        compiler_params=_params("parallel", "arbitrary"),
    )(dz, wd, g, u)


def _rms(x, g):
    r = lax.rsqrt(jnp.mean(x * x, axis=-1, keepdims=True) + RMS_EPS)
    return x * r * g


def _rms_fwd(x, g, name, col_block=0, out_dtype=BF16):
    t = x.shape[0]
    w = g.shape[-1]
    tr = _tile(t, 256)

    def body(x_ref, g_ref, o_ref):
        o_ref[...] = _rms(x_ref[...].astype(F32), g_ref[...]).astype(out_dtype)

    return pl.pallas_call(
        body, name=name, grid=(t // tr,),
        in_specs=[pl.BlockSpec((tr, w), lambda i: (i, col_block)), pl.BlockSpec((1, w), lambda i: (0, 0))],
        out_specs=pl.BlockSpec((tr, w), lambda i: (i, 0)),
        out_shape=jax.ShapeDtypeStruct((t, w), out_dtype),
        compiler_params=_params("parallel"),
    )(x, g.reshape(1, w))


def _post_pre(x, z, g_post, alpha, g_next, name):
    t, w = x.shape
    tr = _tile(t, 256)

    def body(x_ref, z_ref, gp_ref, gn_ref, xo_ref, h_ref):
        xn = x_ref[...] + alpha * _rms(z_ref[...], gp_ref[...])
        xo_ref[...] = xn
        h_ref[...] = _rms(xn, gn_ref[...]).astype(BF16)

    row = pl.BlockSpec((tr, w), lambda i: (i, 0))
    vec = pl.BlockSpec((1, w), lambda i: (0, 0))
    return pl.pallas_call(
        body, name=name, grid=(t // tr,),
        in_specs=[row, row, vec, vec], out_specs=[row, row],
        out_shape=[jax.ShapeDtypeStruct((t, w), F32), jax.ShapeDtypeStruct((t, w), BF16)],
        compiler_params=_params("parallel"),
    )(x, z, g_post.reshape(1, w), g_next.reshape(1, w))


def _post_loss(x, z, g_post, alpha, target, name):
    t, w = x.shape
    tr = _tile(t, 256)

    def body(x_ref, z_ref, gp_ref, t_ref, dy_ref, loss_ref):
        err = x_ref[...] + alpha * _rms(z_ref[...], gp_ref[...]) - t_ref[...]
        dy_ref[...] = err * (1.0 / w)

        @pl.when(pl.program_id(0) == 0)
        def _():
            loss_ref[...] = jnp.zeros_like(loss_ref)

        loss_ref[...] += jnp.sum(jnp.mean(err * err, axis=-1, keepdims=True), axis=0, keepdims=True) * 0.5

    row = pl.BlockSpec((tr, w), lambda i: (i, 0))
    vec = pl.BlockSpec((1, w), lambda i: (0, 0))
    return pl.pallas_call(
        body, name=name, grid=(t // tr,),
        in_specs=[row, row, vec, row], out_specs=[row, pl.BlockSpec((1, 1), lambda i: (0, 0))],
        out_shape=[jax.ShapeDtypeStruct((t, w), F32), jax.ShapeDtypeStruct((1, 1), F32)],
        compiler_params=_params("arbitrary"),
    )(x, z, g_post.reshape(1, w), target)


def _rms_bwd(x, g, dy, alpha, name, res=None, col_block=0, out_dtype=F32):
    t = x.shape[0]
    w = g.shape[-1]
    tr = _tile(t, 256)
    has_res = res is not None

    def body(*refs):
        x_ref, g_ref, dy_ref = refs[:3]
        dx_ref, dg_ref = refs[-2:]
        xx = x_ref[...].astype(F32)
        dyy = dy_ref[...].astype(F32) * alpha
        r = lax.rsqrt(jnp.mean(xx * xx, axis=-1, keepdims=True) + RMS_EPS)
        xh = xx * r
        gy = dyy * g_ref[...]
        dx = r * (gy - xh * jnp.mean(gy * xh, axis=-1, keepdims=True))
        if has_res:
            dx = dx + refs[3][...]
        dx_ref[...] = dx.astype(out_dtype)

        @pl.when(pl.program_id(0) == 0)
        def _():
            dg_ref[...] = jnp.zeros_like(dg_ref)

        dg_ref[...] += jnp.sum(dyy * xh, axis=0, keepdims=True)

    row = pl.BlockSpec((tr, w), lambda i: (i, 0))
    vec = pl.BlockSpec((1, w), lambda i: (0, 0))
    ins = [x, g.reshape(1, w), dy] + ([res] if has_res else [])
    dx, dg = pl.pallas_call(
        body, name=name, grid=(t // tr,),
        in_specs=[pl.BlockSpec((tr, w), lambda i: (i, col_block)), vec, row] + ([row] if has_res else []),
        out_specs=[row, vec],
        out_shape=[jax.ShapeDtypeStruct((t, w), out_dtype), jax.ShapeDtypeStruct((1, w), F32)],
        compiler_params=_params("arbitrary"),
    )(*ins)
    return dx, dg.reshape(w)


def _rot_tables(cos, sin, period, start, fwd):
    t, s = cos.shape
    one = jnp.ones((t, start), F32)
    tail_w = period - start - 2 * s
    tail = jnp.ones((t, tail_w), F32) if start == 0 else jnp.zeros((t, tail_w), F32)
    z = lambda w: jnp.zeros((t, w), F32)
    c = jnp.concatenate([one, cos, cos, tail], axis=1)
    sm = jnp.concatenate([z(start + s), sin, z(tail_w)], axis=1)
    sp = jnp.concatenate([z(start), -sin, z(s + tail_w)], axis=1)
    if fwd:
        return c, sm, sp
    return c, jnp.roll(sp, s, axis=1), jnp.roll(sm, -s, axis=1)


def _rot_apply(x, c, sm, sp, shift):
    w = x.shape[-1]
    return x * c + pltpu.roll(x, shift, 1) * sm + pltpu.roll(x, w - shift, 1) * sp


def _rot(xs, tables, shift, name, col_block=0, width=None, out_dtype=BF16):
    t = xs[0].shape[0]
    w = width if width is not None else xs[0].shape[1]
    tr = _tile(t, 256)
    n_in = len(xs)
    period = tables[0].shape[1] if tables is not None else w

    def body(*refs):
        o_ref = refs[-1]
        for hh in range(w // period):
            sl = slice(hh * period, (hh + 1) * period)
            v = refs[0][:, sl].astype(F32)
            for p in range(1, n_in):
                v = v + refs[p][:, sl].astype(F32)
            if tables is not None:
                c_ref, sm_ref, sp_ref = refs[n_in:n_in + 3]
                v = _rot_apply(v, c_ref[...], sm_ref[...], sp_ref[...], shift)
            o_ref[:, sl] = v.astype(out_dtype)

    tab = pl.BlockSpec((tr, period), lambda i: (i, 0))
    return pl.pallas_call(
        body, name=name, grid=(t // tr,),
        in_specs=[pl.BlockSpec((tr, w), lambda i: (i, col_block))] * n_in + ([tab] * 3 if tables is not None else []),
        out_specs=pl.BlockSpec((tr, w), lambda i: (i, 0)),
        out_shape=jax.ShapeDtypeStruct((t, w), out_dtype),
        compiler_params=_params("parallel"),
    )(*xs, *(tables if tables is not None else ()))


def _mla_kfull(kv, proj, kr_block, tables, heads, name):
    t = kv.shape[0]
    tr = _tile(t, 256)

    def body(kv_ref, kr_ref, c_ref, sm_ref, sp_ref, o_ref):
        kr = _rot_apply(kr_ref[...].astype(F32), c_ref[...], sm_ref[...], sp_ref[...], MLA_ROPE // 2).astype(BF16)
        for hh in range(heads):
            o_ref[:, hh * 256:hh * 256 + 128] = kv_ref[:, hh * 256:hh * 256 + 128]
            o_ref[:, hh * 256 + 128:(hh + 1) * 256] = kr

    tab = pl.BlockSpec((tr, LANES), lambda i: (i, 0))
    full = pl.BlockSpec((tr, heads * 256), lambda i: (i, 0))
    return pl.pallas_call(
        body, name=name, grid=(t // tr,),
        in_specs=[full, pl.BlockSpec((tr, LANES), lambda i: (i, kr_block)), tab, tab, tab],
        out_specs=full, out_shape=jax.ShapeDtypeStruct((t, heads * 256), BF16),
        compiler_params=_params("parallel"),
    )(kv, proj, *tables)


def _mla_dkv(dk, dv, tables, heads, name):
    t = dk.shape[0]
    tr = _tile(t, 256)

    def body(dk_ref, dv_ref, c_ref, sm_ref, sp_ref, dkv_ref, dkr_ref):
        acc = jnp.zeros((tr, LANES), F32)
        for hh in range(heads):
            dkv_ref[:, hh * 256:hh * 256 + 128] = dk_ref[:, hh * 256:hh * 256 + 128].astype(BF16)
            dkv_ref[:, hh * 256 + 128:(hh + 1) * 256] = dv_ref[:, hh * 128:(hh + 1) * 128].astype(BF16)
            acc = acc + dk_ref[:, hh * 256 + 128:(hh + 1) * 256]
        dkr_ref[...] = _rot_apply(acc, c_ref[...], sm_ref[...], sp_ref[...], MLA_ROPE // 2).astype(BF16)

    tab = pl.BlockSpec((tr, LANES), lambda i: (i, 0))
    full = pl.BlockSpec((tr, heads * 256), lambda i: (i, 0))
    return pl.pallas_call(
        body, name=name, grid=(t // tr,),
        in_specs=[full, pl.BlockSpec((tr, heads * 128), lambda i: (i, 0)), tab, tab, tab],
        out_specs=[full, tab],
        out_shape=[jax.ShapeDtypeStruct((t, heads * 256), BF16), jax.ShapeDtypeStruct((t, LANES), BF16)],
        compiler_params=_params("parallel"),
    )(dk, dv, *tables)


def _delta(do, o, name):
    t, w = do.shape
    tr = _tile(t, 256)

    def body(do_ref, o_ref, d_ref):
        for hh in range(w // HEAD_DIM):
            sl = slice(hh * HEAD_DIM, (hh + 1) * HEAD_DIM)
            s = jnp.sum(do_ref[:, sl].astype(F32) * o_ref[:, sl].astype(F32), axis=1, keepdims=True)
            d_ref[:, sl] = jnp.broadcast_to(s, (tr, HEAD_DIM))

    row = pl.BlockSpec((tr, w), lambda i: (i, 0))
    return pl.pallas_call(
        body, name=name, grid=(t // tr,), in_specs=[row, row], out_specs=row,
        out_shape=jax.ShapeDtypeStruct((t, w), F32), compiler_params=_params("parallel"),
    )(do, o)


def _causal_mask(s, row0, col0):
    qi = row0 + lax.broadcasted_iota(jnp.int32, s.shape, 0)
    kj = col0 + lax.broadcasted_iota(jnp.int32, s.shape, 1)
    return jnp.where(kj <= qi, s, NEG)


def _head_group(heads):
    return 2 if heads % 2 == 0 else 1


def _v_specs(rows, hp, row_block):
    return [pl.BlockSpec((rows, MLA_V), functools.partial(lambda h, i, hh: (row_block(i), 2 * (h * hp + hh) + 1), hh=hh))
            for hh in range(hp)]


def _flash_fwd(q, k, kv, heads, name, job=None):
    t = q.shape[0]
    tq = _tile(t, 512)
    hp = _head_group(heads)
    nrep = tq // HEAD_DIM
    scale = 1.0 / math.sqrt(MLA_QK)

    def body(q_ref, k_ref, *rest):
        v_refs = rest[:hp]
        o_ref, lse_ref, m_sc, acc_sc = rest[hp:]
        i = pl.program_id(1)
        m_sc[...] = jnp.full_like(m_sc, NEG)
        acc_sc[...] = jnp.zeros_like(acc_sc)
        ones = jnp.ones((tq, HEAD_DIM), BF16)

        def step(j, masked):
            rows = pl.ds(pl.multiple_of(j * tq, tq), tq)
            for hh in range(hp):
                hs = slice(hh * 256, (hh + 1) * 256)
                s = _dot(q_ref[:, hs], k_ref[rows, hs], _NT) * scale
                if masked:
                    s = _causal_mask(s, 0, 0)
                m_prev = m_sc[hh]
                m_new = jnp.maximum(m_prev, jnp.broadcast_to(jnp.max(s, axis=1, keepdims=True), (tq, HEAD_DIM)))
                a = jnp.exp(m_prev - m_new)
                p = jnp.exp(s - jnp.tile(m_new, (1, nrep)))
                pv = _dot(p, jnp.concatenate([v_refs[hh][rows, :], ones], axis=1), _NN)
                acc_sc[hh] = jnp.tile(a, (1, 2)) * acc_sc[hh] + pv
                m_sc[hh] = m_new

        def loop_body(j, carry):
            step(j, False)
            return carry

        lax.fori_loop(0, i, loop_body, 0)
        step(i, True)
        for hh in range(hp):
            l = acc_sc[hh][:, HEAD_DIM:]
            o_ref[:, hh * MLA_V:(hh + 1) * MLA_V] = (acc_sc[hh][:, :HEAD_DIM] / l).astype(BF16)
            lse_ref[:, hh * HEAD_DIM:(hh + 1) * HEAD_DIM] = m_sc[hh] + jnp.log(l)

    stat = pl.BlockSpec((tq, hp * HEAD_DIM), lambda h, i: (i, h))
    return _carry_call(
        job, body, name, (heads // hp, t // tq),
        [pl.BlockSpec((tq, hp * 256), lambda h, i: (i, h)), pl.BlockSpec((t, hp * 256), lambda h, i: (0, h))]
        + _v_specs(t, hp, lambda i: 0), [stat, stat],
        [jax.ShapeDtypeStruct((t, heads * MLA_V), BF16), jax.ShapeDtypeStruct((t, heads * HEAD_DIM), F32)],
        [pltpu.VMEM((hp, tq, HEAD_DIM), F32), pltpu.VMEM((hp, tq, 2 * HEAD_DIM), F32)],
        ("parallel", "arbitrary"), (q, k, *([kv] * hp)))


def _flash_bwd_dq(q, k, kv, do, lse, delta, heads, name, job=None):
    t = q.shape[0]
    tq = _tile(t, 512)
    hp = _head_group(heads)
    nrep = tq // HEAD_DIM
    scale = 1.0 / math.sqrt(MLA_QK)

    def body(q_ref, k_ref, *rest):
        v_refs = rest[:hp]
        do_ref, lse_ref, d_ref, dq_ref = rest[hp:]
        i = pl.program_id(1)
        dq_ref[...] = jnp.zeros_like(dq_ref)

        def step(j, masked):
            rows = pl.ds(pl.multiple_of(j * tq, tq), tq)
            for hh in range(hp):
                hs = slice(hh * 256, (hh + 1) * 256)
                st = slice(hh * HEAD_DIM, (hh + 1) * HEAD_DIM)
                kk = k_ref[rows, hs]
                s = _dot(q_ref[:, hs], kk, _NT) * scale
                if masked:
                    s = _causal_mask(s, 0, 0)
                p = jnp.exp(s - jnp.tile(lse_ref[:, st], (1, nrep)))
                dp = _dot(do_ref[:, st], v_refs[hh][rows, :], _NT)
                ds = p * (dp - jnp.tile(d_ref[:, st], (1, nrep))) * scale
                dq_ref[:, hs] += _dot(ds, kk, _NN)

        def loop_body(j, carry):
            step(j, False)
            return carry

        lax.fori_loop(0, i, loop_body, 0)
        step(i, True)

    stat = pl.BlockSpec((tq, hp * HEAD_DIM), lambda h, i: (i, h))
    qs = pl.BlockSpec((tq, hp * 256), lambda h, i: (i, h))
    return _carry_call(
        job, body, name, (heads // hp, t // tq),
        [qs, pl.BlockSpec((t, hp * 256), lambda h, i: (0, h))] + _v_specs(t, hp, lambda i: 0) + [stat, stat, stat],
        [qs], [jax.ShapeDtypeStruct((t, heads * 256), F32)], [], ("parallel", "arbitrary"),
        (q, k, *([kv] * hp), do, lse, delta))


def _stat_rows(stat, heads):
    return jnp.transpose(stat[:, ::HEAD_DIM]).reshape(heads, 1, stat.shape[0])


def _flash_bwd_dkv(q, k, kv, do, lse_t, delta_t, heads, name, job=None):
    t = q.shape[0]
    tq = _tile(t, 512)
    nq = t // tq
    hp = _head_group(heads)
    scale = 1.0 / math.sqrt(MLA_QK)

    def body(q_ref, k_ref, *rest):
        v_refs = rest[:hp]
        do_ref, lse_ref, d_ref, dk_ref, dv_ref = rest[hp:]
        j = pl.program_id(1)
        dk_ref[...] = jnp.zeros_like(dk_ref)
        dv_ref[...] = jnp.zeros_like(dv_ref)

        def step(i, masked):
            rows = pl.ds(pl.multiple_of(i * tq, tq), tq)
            for hh in range(hp):
                hs = slice(hh * 256, (hh + 1) * 256)
                st = slice(hh * HEAD_DIM, (hh + 1) * HEAD_DIM)
                qq = q_ref[rows, hs]
                dd = do_ref[rows, st]
                s = _dot(k_ref[:, hs], qq, _NT) * scale
                if masked:
                    kj = lax.broadcasted_iota(jnp.int32, s.shape, 0)
                    qi = lax.broadcasted_iota(jnp.int32, s.shape, 1)
                    s = jnp.where(kj <= qi, s, NEG)
                p = jnp.exp(s - lse_ref[hh, :, rows])
                dv_ref[:, st] += _dot(p, dd, _NN)
                dp = _dot(v_refs[hh][...], dd, _NT)
                ds = p * (dp - d_ref[hh, :, rows]) * scale
                dk_ref[:, hs] += _dot(ds, qq, _NN)

        def loop_body(i, carry):
            step(i, False)
            return carry

        step(j, True)
        lax.fori_loop(j + 1, nq, loop_body, 0)

    ks = pl.BlockSpec((tq, hp * 256), lambda h, j: (j, h))
    row = pl.BlockSpec((hp, 1, t), lambda h, j: (h, 0, 0))
    return _carry_call(
        job, body, name, (heads // hp, nq),
        [pl.BlockSpec((t, hp * 256), lambda h, j: (0, h)), ks] + _v_specs(tq, hp, lambda j: j)
        + [pl.BlockSpec((t, hp * MLA_V), lambda h, j: (0, h)), row, row],
        [ks, pl.BlockSpec((tq, hp * MLA_V), lambda h, j: (j, h))],
        [jax.ShapeDtypeStruct((t, heads * 256), F32), jax.ShapeDtypeStruct((t, heads * MLA_V), F32)], [],
        ("parallel", "arbitrary"), (q, k, *([kv] * hp), do, lse_t, delta_t))


def _dil_view(a, d):
    t, w = a.shape
    return a.reshape(t // d, d * w)


def _dil_band(first_block):
    qi = lax.broadcasted_iota(jnp.int32, (DIL_BLK, 2 * DIL_BLK), 0)
    kj = lax.broadcasted_iota(jnp.int32, (DIL_BLK, 2 * DIL_BLK), 1)
    prev = jnp.logical_and(jnp.logical_and(kj < DIL_BLK, kj >= qi), jnp.logical_not(first_block))
    return jnp.logical_or(prev, jnp.logical_and(kj >= DIL_BLK, kj - DIL_BLK <= qi))


def _dil_fwd(q, k, v, d, name):
    t, w = q.shape
    nb = t // d // DIL_BLK
    scale = 1.0 / math.sqrt(HEAD_DIM)

    def body(q_ref, kp_ref, kc_ref, vp_ref, vc_ref, o_ref, lse_ref):
        mask = _dil_band(pl.program_id(1) == 0)
        ones = jnp.ones((2 * DIL_BLK, HEAD_DIM), BF16)
        for hh in range(w // HEAD_DIM):
            sl = slice(hh * HEAD_DIM, (hh + 1) * HEAD_DIM)
            kk = jnp.concatenate([kp_ref[:, sl], kc_ref[:, sl]], axis=0)
            vv = jnp.concatenate([jnp.concatenate([vp_ref[:, sl], vc_ref[:, sl]], axis=0), ones], axis=1)
            s = jnp.where(mask, _dot(q_ref[:, sl], kk, _NT) * scale, NEG)
            m = jnp.broadcast_to(jnp.max(s, axis=1, keepdims=True), (DIL_BLK, HEAD_DIM))
            pv = _dot(jnp.exp(s - jnp.tile(m, (1, 2))), vv, _NN)
            l = pv[:, HEAD_DIM:]
            o_ref[:, sl] = (pv[:, :HEAD_DIM] / l).astype(BF16)
            lse_ref[:, sl] = m + jnp.log(l)

    cur = pl.BlockSpec((DIL_BLK, w), lambda r, n: (n, r))
    prev = pl.BlockSpec((DIL_BLK, w), lambda r, n: (jnp.maximum(n - 1, 0), r))
    o, lse = pl.pallas_call(
        body, name=name, grid=(d, nb),
        in_specs=[cur, prev, cur, prev, cur], out_specs=[cur, cur],
        out_shape=[jax.ShapeDtypeStruct((t // d, d * w), BF16), jax.ShapeDtypeStruct((t // d, d * w), F32)],
        compiler_params=_params("parallel", "arbitrary"),
    )(_dil_view(q, d), _dil_view(k, d), _dil_view(k, d), _dil_view(v, d), _dil_view(v, d))
    return o.reshape(t, w), lse.reshape(t, w)


def _dil_combine(os_, lses, name):
    t, w = os_[0].shape
    tr = _tile(t, 256)
    n = len(os_)

    def body(*refs):
        o_ref, m_ref = refs[-2:]
        ls = [refs[n + p][...] for p in range(n)]
        m = functools.reduce(jnp.maximum, ls)
        es = [jnp.exp(l - m) for l in ls]
        den = functools.reduce(jnp.add, es)
        acc = es[0] / den * refs[0][...].astype(F32)
        for p in range(1, n):
            acc = acc + es[p] / den * refs[p][...].astype(F32)
        o_ref[...] = acc.astype(BF16)
        m_ref[...] = m + jnp.log(den)

    row = pl.BlockSpec((tr, w), lambda i: (i, 0))
    return pl.pallas_call(
        body, name=name, grid=(t // tr,), in_specs=[row] * (2 * n), out_specs=[row, row],
        out_shape=[jax.ShapeDtypeStruct((t, w), BF16), jax.ShapeDtypeStruct((t, w), F32)],
        compiler_params=_params("parallel"),
    )(*os_, *lses)


def _dil_bwd_dq(q, k, v, do, mt, delta, d, name):
    t, w = q.shape
    nb = t // d // DIL_BLK
    scale = 1.0 / math.sqrt(HEAD_DIM)

    def body(q_ref, kp_ref, kc_ref, vp_ref, vc_ref, do_ref, mt_ref, d_ref, dq_ref):
        mask = _dil_band(pl.program_id(1) == 0)
        for hh in range(w // HEAD_DIM):
            sl = slice(hh * HEAD_DIM, (hh + 1) * HEAD_DIM)
            kk = jnp.concatenate([kp_ref[:, sl], kc_ref[:, sl]], axis=0)
            vv = jnp.concatenate([vp_ref[:, sl], vc_ref[:, sl]], axis=0)
            s = jnp.where(mask, _dot(q_ref[:, sl], kk, _NT) * scale, NEG)
            p = jnp.exp(s - jnp.tile(mt_ref[:, sl], (1, 2)))
            ds = p * (_dot(do_ref[:, sl], vv, _NT) - jnp.tile(d_ref[:, sl], (1, 2))) * scale
            dq_ref[:, sl] = _dot(ds, kk, _NN).astype(BF16)

    cur = pl.BlockSpec((DIL_BLK, w), lambda r, n: (n, r))
    prev = pl.BlockSpec((DIL_BLK, w), lambda r, n: (jnp.maximum(n - 1, 0), r))
    kv_, vv_ = _dil_view(k, d), _dil_view(v, d)
    dq = pl.pallas_call(
        body, name=name, grid=(d, nb),
        in_specs=[cur, prev, cur, prev, cur, cur, cur, cur], out_specs=cur,
        out_shape=jax.ShapeDtypeStruct((t // d, d * w), BF16),
        compiler_params=_params("parallel", "arbitrary"),
    )(_dil_view(q, d), kv_, kv_, vv_, vv_, _dil_view(do, d), _dil_view(mt, d), _dil_view(delta, d))
    return dq.reshape(t, w)


def _dil_bwd_dkv(q, k, v, do, mt, delta, d, name):
    t, w = q.shape
    nb = t // d // DIL_BLK
    scale = 1.0 / math.sqrt(HEAD_DIM)

    def body(k_ref, v_ref, qs_ref, qn_ref, dos_ref, don_ref, ms_ref, mn_ref, ds_ref, dn_ref, dk_ref, dv_ref):
        last = pl.program_id(1) == nb - 1
        qi = lax.broadcasted_iota(jnp.int32, (2 * DIL_BLK, DIL_BLK), 0)
        kj = lax.broadcasted_iota(jnp.int32, (2 * DIL_BLK, DIL_BLK), 1)
        nxt = jnp.logical_and(jnp.logical_and(qi >= DIL_BLK, kj >= qi - DIL_BLK), jnp.logical_not(last))
        mask = jnp.logical_or(jnp.logical_and(qi < DIL_BLK, kj <= qi), nxt)
        for hh in range(w // HEAD_DIM):
            sl = slice(hh * HEAD_DIM, (hh + 1) * HEAD_DIM)
            qq = jnp.concatenate([qs_ref[:, sl], qn_ref[:, sl]], axis=0)
            dd = jnp.concatenate([dos_ref[:, sl], don_ref[:, sl]], axis=0)
            mm = jnp.concatenate([ms_ref[:, sl], mn_ref[:, sl]], axis=0)
            de = jnp.concatenate([ds_ref[:, sl], dn_ref[:, sl]], axis=0)
            s = jnp.where(mask, _dot(qq, k_ref[:, sl], _NT) * scale, NEG)
            p = jnp.exp(s - mm)
            dv_ref[:, sl] = _dot(p, dd, _TN).astype(BF16)
            dsv = p * (_dot(dd, v_ref[:, sl], _NT) - de) * scale
            dk_ref[:, sl] = _dot(dsv, qq, _TN).astype(BF16)

    cur = pl.BlockSpec((DIL_BLK, w), lambda r, n: (n, r))
    nxt = pl.BlockSpec((DIL_BLK, w), lambda r, n: (jnp.minimum(n + 1, nb - 1), r))
    shape = jax.ShapeDtypeStruct((t // d, d * w), BF16)
    qv, dov, mv, dv_ = _dil_view(q, d), _dil_view(do, d), _dil_view(mt, d), _dil_view(delta, d)
    dk, dv = pl.pallas_call(
        body, name=name, grid=(d, nb),
        in_specs=[cur, cur, cur, nxt, cur, nxt, cur, nxt, cur, nxt], out_specs=[cur, cur], out_shape=[shape, shape],
        compiler_params=_params("parallel", "arbitrary"),
    )(_dil_view(k, d), _dil_view(v, d), qv, qv, dov, dov, mv, mv, dv_, dv_)
    return dk.reshape(t, w), dv.reshape(t, w)


def _adamw(w, g, m, v, name):
    r, c = w.shape
    tr = _tile(r, 256, 8)
    c1 = 1.0 - ADAM_B1 ** ADAM_STEP
    c2 = 1.0 - ADAM_B2 ** ADAM_STEP

    def body(w_ref, g_ref, m_ref, v_ref, d_ref, mo_ref, vo_ref):
        gg = g_ref[...]
        mn = ADAM_B1 * m_ref[...] + (1.0 - ADAM_B1) * gg
        vn = ADAM_B2 * v_ref[...] + (1.0 - ADAM_B2) * (gg * gg)
        mo_ref[...] = mn
        vo_ref[...] = vn
        d_ref[...] = -ADAM_LR * ((mn / c1) / (jnp.sqrt(vn / c2) + ADAM_EPS) + ADAM_WD * w_ref[...])

    row = pl.BlockSpec((tr, c), lambda i: (i, 0))
    shape = jax.ShapeDtypeStruct((r, c), F32)
    return pl.pallas_call(
        body, name=name, grid=(r // tr,), in_specs=[row] * 4, out_specs=[row] * 3, out_shape=[shape] * 3,
        compiler_params=_params("parallel"),
    )(w, g, m, v)


def _position():
    return lax.axis_index("x"), lax.axis_index("y"), lax.axis_index("c")


def _other_chips(x, y):
    return [(1 - x, y), (x, 1 - y), (1 - x, 1 - y)]


_ANY = pl.BlockSpec(memory_space=pl.ANY)


def _half_rows(a, half):
    return pl.ds(pl.multiple_of(half * (a // 2), 16), a // 2)


class _GatherJob:
    def __init__(self, shards):
        self.inputs = list(shards)
        n = len(shards)
        self.out_shapes = [jax.ShapeDtypeStruct((N_CHIPS,) + s.shape, s.dtype) for s in shards]
        self.scratch = [pltpu.SemaphoreType.DMA((2, 3, n)), pltpu.SemaphoreType.DMA((2, 3, n)),
                        pltpu.SemaphoreType.DMA((n,))]

    def _copies(self, w_refs, out_refs, sems):
        send_sems, recv_sems, local_sems = sems
        x, y, c = _position()
        me = 2 * x + y
        n = len(self.inputs)

        def copy(level, kk, i, src, dst, to):
            return functools.partial(
                pltpu.make_async_remote_copy, src_ref=src, dst_ref=dst, send_sem=send_sems.at[level, kk, i],
                recv_sem=recv_sems.at[level, kk, i], device_id=to, device_id_type=MESH)

        mine = [functools.partial(pltpu.make_async_copy, w_refs[i], out_refs[i].at[me], local_sems.at[i]) for i in range(n)]
        sends, landed, passed = [], [], []
        for i in range(n):
            a = self.inputs[i].shape[0]
            for kk, (px, py) in enumerate(_other_chips(x, y)):
                rows = _half_rows(a, c)
                sends.append(copy(0, kk, i, w_refs[i].at[rows], out_refs[i].at[me, rows], (px, py, c)))
                got = out_refs[i].at[2 * px + py, rows]
                landed.append((copy(0, kk, i, got, got, (px, py, c)), copy(1, kk, i, got, got, (x, y, 1 - c))))
                got = out_refs[i].at[2 * px + py, _half_rows(a, 1 - c)]
                passed.append(copy(1, kk, i, got, got, (x, y, 1 - c)))
        return mine, sends, landed, passed

    def start(self, w_refs, out_refs, sems):
        mine, sends, _, _ = self._copies(w_refs, out_refs, sems)
        for make in sends + mine:
            make().start()

    def finish(self, w_refs, out_refs, sems):
        mine, sends, landed, passed = self._copies(w_refs, out_refs, sems)
        forwards = []
        for arrival, forward in landed:
            arrival().wait_recv()
            forwards.append(forward())
            forwards[-1].start()
        for make in passed:
            make().wait_recv()
        for make in sends:
            make().wait_send()
        for cp in forwards:
            cp.wait_send()
        for make in mine:
            make().wait()


class _ExchangeJob:
    def __init__(self, sums):
        self.inputs = list(sums)
        n = len(sums)
        self.out_shapes = [jax.ShapeDtypeStruct((3,) + s.shape[1:], s.dtype) for s in sums]
        self.scratch = [pltpu.SemaphoreType.DMA((n, 3)), pltpu.SemaphoreType.DMA((n, 3))]

    def _copies(self, s_refs, out_refs, sems):
        send_sems, recv_sems = sems
        x, y, c = _position()
        return [pltpu.make_async_remote_copy(
            src_ref=s_refs[i].at[2 * px + py], dst_ref=out_refs[i].at[kk], send_sem=send_sems.at[i, kk],
            recv_sem=recv_sems.at[i, kk], device_id=(px, py, c), device_id_type=MESH)
            for i in range(len(self.inputs)) for kk, (px, py) in enumerate(_other_chips(x, y))]

    def start(self, s_refs, out_refs, sems):
        for cp in self._copies(s_refs, out_refs, sems):
            cp.start()

    def finish(self, s_refs, out_refs, sems):
        cps = self._copies(s_refs, out_refs, sems)
        for cp in cps:
            cp.wait_recv()
        for cp in cps:
            cp.wait_send()


def _run_job(job, name):
    ni, no = len(job.inputs), len(job.out_shapes)

    def body(*refs):
        job.start(refs[:ni], refs[ni:ni + no], refs[ni + no:])
        job.finish(refs[:ni], refs[ni:ni + no], refs[ni + no:])

    return pl.pallas_call(body, name=name, in_specs=[_ANY] * ni, out_specs=[_ANY] * no, out_shape=job.out_shapes,
                          scratch_shapes=job.scratch)(*job.inputs)


def _carry(job, body, n_in, n_out, grid):
    if job is None:
        return body
    ni, no, ns = len(job.inputs), len(job.out_shapes), len(job.scratch)

    def carried(*refs):
        own_in, job_in = refs[:n_in], refs[n_in:n_in + ni]
        own_out = refs[n_in + ni:n_in + ni + n_out]
        job_out = refs[n_in + ni + n_out:n_in + ni + n_out + no]
        scratch = refs[n_in + ni + n_out + no:]
        own_scratch, job_scratch = scratch[:len(scratch) - ns], scratch[len(scratch) - ns:]
        ids = [pl.program_id(ax) for ax in range(len(grid))]
        first = functools.reduce(jnp.logical_and, [i == 0 for i in ids])
        last = functools.reduce(jnp.logical_and, [i == g - 1 for i, g in zip(ids, grid)])

        @pl.when(first)
        def _():
            job.start(job_in, job_out, job_scratch)

        body(*own_in, *own_out, *own_scratch)

        @pl.when(last)
        def _():
            job.finish(job_in, job_out, job_scratch)

    return carried


def _carry_call(job, body, name, grid, in_specs, out_specs, out_shape, scratch_shapes, sem, args):
    n_out = len(out_shape)
    if job is None:
        outs = pl.pallas_call(body, name=name, grid=grid, in_specs=in_specs, out_specs=out_specs, out_shape=out_shape,
                              scratch_shapes=scratch_shapes, compiler_params=_params(*sem))(*args)
        return list(outs), None
    outs = pl.pallas_call(
        _carry(job, body, len(in_specs), n_out, grid), name=name, grid=grid,
        in_specs=list(in_specs) + [_ANY] * len(job.inputs), out_specs=list(out_specs) + [_ANY] * len(job.out_shapes),
        out_shape=list(out_shape) + job.out_shapes, scratch_shapes=list(scratch_shapes) + job.scratch,
        compiler_params=_params(*(["arbitrary"] * len(grid))),
    )(*args, *job.inputs)
    return list(outs[:n_out]), list(outs[n_out:])


def _swap_halves(gs, name):
    n = len(gs)

    def body(*refs):
        g_refs, out_refs = refs[:n], refs[n:2 * n]
        send_sems, recv_sems = refs[2 * n:]
        x, y, c = _position()
        cps = []
        for i in range(n):
            cp = pltpu.make_async_remote_copy(
                src_ref=g_refs[i].at[:, _half_rows(gs[i].shape[1], 1 - c)], dst_ref=out_refs[i],
                send_sem=send_sems.at[i], recv_sem=recv_sems.at[i], device_id=(x, y, 1 - c), device_id_type=MESH)
            cp.start()
            cps.append(cp)
        for cp in cps:
            cp.wait_recv()
        for cp in cps:
            cp.wait_send()

    return pl.pallas_call(
        body, name=name, in_specs=[_ANY] * n, out_specs=[_ANY] * n,
        out_shape=[jax.ShapeDtypeStruct((g.shape[0], g.shape[1] // 2, g.shape[2]), g.dtype) for g in gs],
        scratch_shapes=[pltpu.SemaphoreType.DMA((n,)), pltpu.SemaphoreType.DMA((n,))],
    )(*gs)


def _share_halves(bufs, name):
    n = len(bufs)

    def body(*refs):
        in_refs, out_refs = refs[:n], refs[n:2 * n]
        send_sems, recv_sems = refs[2 * n:]
        x, y, c = _position()
        cps = []
        for i in range(n):
            cp = pltpu.make_async_remote_copy(
                src_ref=in_refs[i].at[c], dst_ref=out_refs[i].at[c], send_sem=send_sems.at[i], recv_sem=recv_sems.at[i],
                device_id=(x, y, 1 - c), device_id_type=MESH)
            cp.start()
            cps.append(cp)
        for i in range(n):
            pltpu.make_async_remote_copy(
                src_ref=in_refs[i].at[1 - c], dst_ref=out_refs[i].at[1 - c], send_sem=send_sems.at[i],
                recv_sem=recv_sems.at[i], device_id=(x, y, 1 - c), device_id_type=MESH).wait_recv()
        for cp in cps:
            cp.wait_send()

    return pl.pallas_call(
        body, name=name, in_specs=[_ANY] * n, out_specs=[_ANY] * n,
        out_shape=[jax.ShapeDtypeStruct(b.shape, b.dtype) for b in bufs],
        input_output_aliases={i: i for i in range(n)},
        scratch_shapes=[pltpu.SemaphoreType.DMA((n,)), pltpu.SemaphoreType.DMA((n,))],
    )(*bufs)


def _add_pair(g, got, name):
    n, a, b = g.shape
    ah = a // 2
    tr = _tile(ah, 256, 16)

    def body(c_ref, a_ref, b_ref, o_ref):
        o_ref[...] = (a_ref[...].astype(F32) + b_ref[...].astype(F32)).astype(BF16)

    blk = pl.BlockSpec((None, tr, b), lambda j, i, c_ref: (j, i, 0))
    return pl.pallas_call(
        body, name=name,
        grid_spec=pltpu.PrefetchScalarGridSpec(
            num_scalar_prefetch=1, grid=(n, ah // tr),
            in_specs=[pl.BlockSpec((None, None, tr, b), lambda j, i, c_ref: (j, c_ref[0], i, 0)), blk], out_specs=blk),
        out_shape=jax.ShapeDtypeStruct((n, ah, b), BF16),
        compiler_params=_params("parallel", "parallel"),
    )(lax.axis_index("c").reshape(1).astype(jnp.int32), g.reshape(n, 2, ah, b), got)


def _add_chips(sums, got, name):
    n, ah, b = sums.shape
    tr = _tile(ah, 256, 16)

    def body(at_ref, a_ref, b_ref, o_ref):
        o_ref[...] = ((a_ref[...].astype(F32) + b_ref[0].astype(F32)) + b_ref[1].astype(F32)) + b_ref[2].astype(F32)

    at = jnp.stack([2 * lax.axis_index("x") + lax.axis_index("y"), lax.axis_index("c")]).astype(jnp.int32)
    return pl.pallas_call(
        body, name=name,
        grid_spec=pltpu.PrefetchScalarGridSpec(
            num_scalar_prefetch=1, grid=(ah // tr,),
            in_specs=[pl.BlockSpec((None, tr, b), lambda i, at_ref: (at_ref[0], i, 0)),
                      pl.BlockSpec((3, tr, b), lambda i, at_ref: (0, i, 0))],
            out_specs=pl.BlockSpec((None, tr, b), lambda i, at_ref: (at_ref[1], i, 0))),
        out_shape=jax.ShapeDtypeStruct((2, ah, b), F32),
        compiler_params=_params("parallel"),
    )(at, sums, got)


def _allgather_small(v, name):
    m_per, w = v.shape

    def body(x_ref, out_ref, send_sems, recv_sems, local_sem):
        x, y, c = _position()
        me, sibling = (x, y, c), (x, y, 1 - c)
        chips = _other_chips(x, y)

        def rows(px, py, pc):
            return out_ref.at[pl.ds(pl.multiple_of((4 * px + 2 * py + pc) * m_per, 8), m_per), :]

        def copy(kk, block, to, src=None):
            return pltpu.make_async_remote_copy(
                src_ref=rows(*block) if src is None else src, dst_ref=rows(*block), send_sem=send_sems.at[kk],
                recv_sem=recv_sems.at[kk], device_id=to, device_id_type=MESH)

        mine = pltpu.make_async_copy(x_ref, rows(*me), local_sem)
        mine.start()
        first = [copy(0, me, sibling, src=x_ref)]
        first += [copy(1 + j, me, (*chip, c), src=x_ref) for j, chip in enumerate(chips)]
        for cp in first:
            cp.start()
        passed = [copy(4 + j, (*chip, c), sibling) for j, chip in enumerate(chips)]
        for j, chip in enumerate(chips):
            copy(1 + j, (*chip, c), me).wait_recv()
            passed[j].start()
        copy(0, sibling, me).wait_recv()
        for j, chip in enumerate(chips):
            copy(4 + j, (*chip, 1 - c), me).wait_recv()
        for cp in first + passed:
            cp.wait_send()
        mine.wait()

    return pl.pallas_call(
        body, name=name,
        out_shape=jax.ShapeDtypeStruct((N_DEV * m_per, w), v.dtype),
        in_specs=[pl.BlockSpec(memory_space=pltpu.VMEM)], out_specs=pl.BlockSpec(memory_space=pltpu.VMEM),
        scratch_shapes=[pltpu.SemaphoreType.DMA((7,)), pltpu.SemaphoreType.DMA((7,)), pltpu.SemaphoreType.DMA],
    )(v)


def _sum_blocks(a, n, name):
    m_per, w = a.shape[0] // n, a.shape[1]

    def body(a_ref, o_ref):
        s = a_ref[0:m_per, :]
        for j in range(1, n):
            s = s + a_ref[j * m_per:(j + 1) * m_per, :]
        o_ref[...] = s

    return pl.pallas_call(body, name=name, out_shape=jax.ShapeDtypeStruct((m_per, w), a.dtype))(a)


def _pack_gains(d):
    flat = jnp.concatenate([d[n].reshape(1, -1) for n in GAIN_NAMES], axis=1)
    n = flat.shape[-1]
    rows = -(-n // (PACK_W * 8)) * 8
    return jnp.pad(flat, ((0, 0), (0, rows * PACK_W - n))).reshape(rows, PACK_W)


def _to_slabs(g):
    a, nb = g.shape
    return jnp.transpose(g.reshape(a, N_CHIPS, nb // N_CHIPS), (1, 0, 2))


def _from_slabs(s):
    n, a, b = s.shape
    return jnp.transpose(s, (1, 0, 2)).reshape(a, n * b)


def _unpack_gains(slab, shapes):
    flat = slab.reshape(-1)
    out, off = {}, 0
    for n in GAIN_NAMES:
        a, b = shapes[n]
        out[n] = flat[off:off + a * b].reshape(a, b)
        off += a * b
    return out


def _reduce_begin(gs):
    got = _swap_halves(gs, "rs_swap_halves")
    return _ExchangeJob([_add_pair(g, h, "rs_add_pair") for g, h in zip(gs, got)])


def _reduce_end(job, got):
    halves = [_add_chips(s, h, "rs_add_chips") for s, h in zip(job.inputs, got)]
    return [b.reshape(b.shape[1] * 2, b.shape[2]) for b in _share_halves(halves, "rs_share_halves")]


def _ffn_forward(h, wl, pre, tag, job=None):
    (g, u, a), got = _ffn_up(h, wl[pre + '_w_gate'], wl[pre + '_w_up'], "ffn_up" + tag, job)
    wd = wl[pre + '_w_down']
    z = _mm([(a, wd.reshape(-1, wd.shape[2]))], 'nn', F32, "ffn_down" + tag, tn=1024, tk=wd.shape[1])
    return z, (g, u, a), got


def _ffn_backward(dz, h, saved, wl, pre, grads, tag):
    g, u, a = saved
    wd = wl[pre + '_w_down']
    dg, du = _ffn_bwd_act(dz, wd.reshape(-1, wd.shape[2]), g, u, "ffn_bwd_act" + tag)
    grads[pre + '_w_down'] = _mm([(a, dz)], 'tn', BF16, "ffn_dw_down" + tag, tm=wd.shape[1], tn=1024, tk=1024).reshape(wd.shape)
    grads[pre + '_w_gate'] = _mm([(h, dg)], 'tn', BF16, "ffn_dw_gate" + tag, tk=1024, out_slabs=N_CHIPS)
    grads[pre + '_w_up'] = _mm([(h, du)], 'tn', BF16, "ffn_dw_up" + tag, tk=1024, out_slabs=N_CHIPS)
    return _mm([(dg, wl[pre + '_w_gate']), (du, wl[pre + '_w_up'])], 'nt', F32, "ffn_dh" + tag, tn=1024, b_slabs=True)


def kernel(x, positions, ffn1_pre_g, ffn1_post_g, ffn1_w_gate, ffn1_w_up, ffn1_w_down, mix_pre_g, mix_post_g, w_in, mla_q_norm_g, mla_w_uq, mla_kv_norm_g, mla_w_ukv, w_o, ffn2_pre_g, ffn2_post_g, ffn2_w_gate, ffn2_w_up, ffn2_w_down, loss_target, m_ffn1_pre_g, m_ffn1_post_g, m_ffn1_w_gate, m_ffn1_w_up, m_ffn1_w_down, m_mix_pre_g, m_mix_post_g, m_w_in, m_mla_q_norm_g, m_mla_w_uq, m_mla_kv_norm_g, m_mla_w_ukv, m_w_o, m_ffn2_pre_g, m_ffn2_post_g, m_ffn2_w_gate, m_ffn2_w_up, m_ffn2_w_down, v_ffn1_pre_g, v_ffn1_post_g, v_ffn1_w_gate, v_ffn1_w_up, v_ffn1_w_down, v_mix_pre_g, v_mix_post_g, v_w_in, v_mla_q_norm_g, v_mla_w_uq, v_mla_kv_norm_g, v_mla_w_ukv, v_w_o, v_ffn2_pre_g, v_ffn2_post_g, v_ffn2_w_gate, v_ffn2_w_up, v_ffn2_w_down):
    given = dict(locals())
    ws = {n: given[n] for n in WEIGHT_NAMES}
    ms = {n: given['m_' + n] for n in WEIGHT_NAMES}
    vs = {n: given['v_' + n] for n in WEIGHT_NAMES}

    depth = ffn1_pre_g.shape[0]
    t, d_model = x.shape[1], x.shape[2]
    rq, rkv = mla_q_norm_g.shape[1], mla_kv_norm_g.shape[1]
    h_mla = mla_w_uq.shape[2] * N_CHIPS // MLA_QK
    wd_ = (w_in.shape[2] * N_CHIPS - rq - rkv - MLA_ROPE) // 3
    assert rq == rkv and (rq + rkv) % wd_ == 0 and wd_ % HEAD_DIM == 0
    assert mla_w_ukv.shape[2] * N_CHIPS == h_mla * 256 and w_o.shape[1] * N_CHIPS == h_mla * MLA_V + wd_
    lat = rq + rkv
    in_p = lat + 3 * wd_ + LANES
    qd_block = lat // wd_
    kr_block = (lat + 3 * wd_) // LANES
    gain_shapes = {n: ws[n].shape for n in GAIN_NAMES}

    pos = positions[0].astype(F32)[:, None]

    def cos_sin(dim):
        inv = ROPE_THETA ** (-jnp.arange(0, dim, 2, dtype=F32) / dim)
        return jnp.cos(pos * inv), jnp.sin(pos * inv)

    cos_a, sin_a = cos_sin(MLA_ROPE)
    cos_p, sin_p = cos_sin(PARTIAL_ROPE)
    tab_q = [_rot_tables(cos_a, sin_a, MLA_QK_PAD, MLA_NOPE, f) for f in (True, False)]
    tab_kr = [_rot_tables(cos_a, sin_a, LANES, 0, f) for f in (True, False)]
    tab_d = [_rot_tables(cos_p, sin_p, HEAD_DIM, 0, f) for f in (True, False)]
    s_a, s_p = MLA_ROPE // 2, PARTIAL_ROPE // 2

    w16 = {n: ws[n].astype(BF16) for n in MATRIX_NAMES}
    parts = {'ffn1': [n for n in MATRIX_NAMES if n.startswith('ffn1')], 'ffn2': [n for n in MATRIX_NAMES if n.startswith('ffn2')],
             'mix': [n for n in MATRIX_NAMES if not n.startswith('ffn')]}

    def gather_job(l, part):
        return _GatherJob([w16[n][l] for n in parts[part]]) if l < depth else None

    layers = [{} for _ in range(depth)]
    layers[0].update(zip(parts['ffn1'], _run_job(gather_job(0, 'ffn1'), "allgather_weights")))

    xc = x[0]
    h = _rms_fwd(xc, ffn1_pre_g[0], "rms_first")
    saved = []
    dy = loss = None
    for l in range(depth):
        wl, sv = layers[l], {}
        sv['x0'], sv['h1'] = xc, h
        z, sv['ffn1'], got = _ffn_forward(h, wl, 'ffn1', "_1", gather_job(l, 'mix'))
        wl.update(zip(parts['mix'], got))
        w_in_l = _from_slabs(wl['w_in'])
        wl['w_in'] = jnp.concatenate([w_in_l[:, :lat], w_in_l[:, lat + MLA_ROPE:], w_in_l[:, lat:lat + MLA_ROPE],
                                      jnp.zeros((d_model, LANES - MLA_ROPE), BF16)], axis=1)
        wl['mla_w_uq'] = jnp.pad(_from_slabs(wl['mla_w_uq']).reshape(rq, h_mla, MLA_QK),
                                 ((0, 0), (0, 0), (0, MLA_QK_PAD - MLA_QK))).reshape(rq, h_mla * MLA_QK_PAD)
        wl['w_o'] = wl['w_o'].reshape(-1, d_model)
        sv['z1'] = z
        xc, h = _post_pre(xc, z, ffn1_post_g[l], 0.5, mix_pre_g[l], "post_pre_1")
        sv['x1'], sv['h2'] = xc, h

        proj = _mm([(h, wl['w_in'])], 'nn', BF16, "mix_in", tn=1408, tk=d_model)
        cqn = _rms_fwd(proj, mla_q_norm_g[l], "rms_cq", col_block=0)
        ckvn = _rms_fwd(proj, mla_kv_norm_g[l], "rms_ckv", col_block=1)
        qf = _rot([_mm([(cqn, wl['mla_w_uq'])], 'nn', F32, "mla_uq")], tab_q[0], s_a, "rope_q")
        kv = _mm([(ckvn, wl['mla_w_ukv'])], 'nn', BF16, "mla_ukv", b_slabs=True)
        kf = _mla_kfull(kv, proj, kr_block, tab_kr[0], h_mla, "mla_kfull")
        (oa, lse), got = _flash_fwd(qf, kf, kv, h_mla, "mla_attn", gather_job(l, 'ffn2'))
        wl.update(zip(parts['ffn2'], got))
        qd = _rot([proj], tab_d[0], s_p, "rope_qd", col_block=qd_block, width=wd_)
        kd = _rot([proj], tab_d[0], s_p, "rope_kd", col_block=qd_block + 1, width=wd_)
        vd = proj[:, lat + 2 * wd_:lat + 3 * wd_]
        outs = [_dil_fwd(qd, kd, vd, dil, "dil_attn_%d" % dil) for _, dil in DIL_PATTERNS]
        ob, mt = _dil_combine([o for o, _ in outs], [s for _, s in outs], "dil_combine")
        o = jnp.concatenate([oa, ob], axis=1)
        z = _mm([(o, wl['w_o'])], 'nn', F32, "mix_out", tk=o.shape[1])
        sv.update(proj=proj, cqn=cqn, ckvn=ckvn, qf=qf, kv=kv, kf=kf, oa=oa, lse=lse, qd=qd, kd=kd, vd=vd, ob=ob,
                  mt=mt, o=o, z2=z)
        xc, h = _post_pre(xc, z, mix_post_g[l], 1.0, ffn2_pre_g[l], "post_pre_2")
        sv['x2'], sv['h3'] = xc, h

        z, sv['ffn2'], got = _ffn_forward(h, wl, 'ffn2', "_2", gather_job(l + 1, 'ffn1'))
        sv['z3'] = z
        if got is not None:
            layers[l + 1].update(zip(parts['ffn1'], got))
        if l + 1 < depth:
            xc, h = _post_pre(xc, z, ffn2_post_g[l], 0.5, ffn1_pre_g[l + 1], "post_pre_3")
        else:
            dy, loss = _post_loss(xc, z, ffn2_post_g[l], 0.5, loss_target[0], "post_loss")
        saved.append(sv)

    dx = dy
    gain_grads = {n: [None] * depth for n in GAIN_NAMES}
    shard_grads = [{} for _ in range(depth)]
    late_names = parts['mix'] + parts['ffn1']
    late = None
    for l in reversed(range(depth)):
        wl, sv, grads = layers[l], saved[l], {}

        dz, gain_grads['ffn2_post_g'][l] = _rms_bwd(sv['z3'], ffn2_post_g[l], dx, 0.5, "rms_bwd_post_f", out_dtype=BF16)
        dh = _ffn_backward(dz, sv['h3'], sv['ffn2'], wl, 'ffn2', grads, "_2")
        dx, gain_grads['ffn2_pre_g'][l] = _rms_bwd(sv['x2'], ffn2_pre_g[l], dh, 1.0, "rms_bwd_pre", res=dx)
        early = _reduce_begin([grads[n] for n in parts['ffn2']])

        dz, gain_grads['mix_post_g'][l] = _rms_bwd(sv['z2'], mix_post_g[l], dx, 1.0, "rms_bwd_post_m", out_dtype=BF16)
        do = _mm([(dz, wl['w_o'])], 'nt', BF16, "mix_out_dx", tk=d_model)
        grads['w_o'] = _mm([(sv['o'], dz)], 'tn', BF16, "mix_out_dw", tn=1024, tk=1024).reshape(N_CHIPS, -1, d_model)
        do_a, do_b = do[:, :h_mla * MLA_V], do[:, h_mla * MLA_V:]

        delta = _delta(do_a, sv['oa'], "delta_a")
        (dqf,), got = _flash_bwd_dq(sv['qf'], sv['kf'], sv['kv'], do_a, sv['lse'], delta, h_mla, "mla_attn_dq", early)
        shard_grads[l].update(zip(parts['ffn2'], _reduce_end(early, got)))
        (dkf, dva), got = _flash_bwd_dkv(sv['qf'], sv['kf'], sv['kv'], do_a, _stat_rows(sv['lse'], h_mla),
                                         _stat_rows(delta, h_mla), h_mla, "mla_attn_dkv", late[1] if late else None)
        if late:
            shard_grads[late[0]].update(zip(late_names, _reduce_end(late[1], got)))
        dq = _rot([dqf], tab_q[1], s_a, "rope_q_bwd")
        grads['mla_w_uq'] = _mm([(sv['cqn'], dq)], 'tn', BF16, "mla_uq_dw", tk=1024)
        dcqn = _mm([(dq, wl['mla_w_uq'])], 'nt', F32, "mla_uq_dx", tk=1024)
        dcq, gain_grads['mla_q_norm_g'][l] = _rms_bwd(sv['proj'], mla_q_norm_g[l], dcqn, 1.0, "rms_bwd_cq",
                                                      col_block=0, out_dtype=BF16)
        dkv, dkr = _mla_dkv(dkf, dva, tab_kr[1], h_mla, "mla_dkv")
        grads['mla_w_ukv'] = _mm([(sv['ckvn'], dkv)], 'tn', BF16, "mla_ukv_dw", tk=1024, out_slabs=N_CHIPS)
        dckvn = _mm([(dkv, wl['mla_w_ukv'])], 'nt', F32, "mla_ukv_dx", b_slabs=True)
        dckv, gain_grads['mla_kv_norm_g'][l] = _rms_bwd(sv['proj'], mla_kv_norm_g[l], dckvn, 1.0, "rms_bwd_ckv",
                                                        col_block=1, out_dtype=BF16)

        delta = _delta(do_b, sv['ob'], "delta_b")
        dqs, dks, dvs = [], [], []
        for _, dil in DIL_PATTERNS:
            dqs.append(_dil_bwd_dq(sv['qd'], sv['kd'], sv['vd'], do_b, sv['mt'], delta, dil, "dil_attn_dq_%d" % dil))
            dk_, dv_ = _dil_bwd_dkv(sv['qd'], sv['kd'], sv['vd'], do_b, sv['mt'], delta, dil, "dil_attn_dkv_%d" % dil)
            dks.append(dk_)
            dvs.append(dv_)
        dqd = _rot(dqs, tab_d[1], s_p, "rope_qd_bwd")
        dkd = _rot(dks, tab_d[1], s_p, "rope_kd_bwd")
        dvd = _rot(dvs, None, 0, "sum_dvd")
        dproj = jnp.concatenate([dcq, dckv, dqd, dkd, dvd, dkr], axis=1)
        grads['w_in'] = _mm([(sv['h2'], dproj)], 'tn', BF16, "mix_in_dw", tn=1408, tk=1024)
        dh = _mm([(dproj, wl['w_in'])], 'nt', F32, "mix_in_dx", tn=1024, tk=1408)
        dx, gain_grads['mix_pre_g'][l] = _rms_bwd(sv['x1'], mix_pre_g[l], dh, 1.0, "rms_bwd_pre", res=dx)

        dz, gain_grads['ffn1_post_g'][l] = _rms_bwd(sv['z1'], ffn1_post_g[l], dx, 0.5, "rms_bwd_post_f", out_dtype=BF16)
        dh = _ffn_backward(dz, sv['h1'], sv['ffn1'], wl, 'ffn1', grads, "_1")
        dx, gain_grads['ffn1_pre_g'][l] = _rms_bwd(sv['x0'], ffn1_pre_g[l], dh, 1.0, "rms_bwd_pre", res=dx)

        gp = grads['w_in']
        grads['w_in'] = _to_slabs(jnp.concatenate(
            [gp[:, :lat], gp[:, in_p - LANES:in_p - LANES + MLA_ROPE], gp[:, lat:lat + 3 * wd_]], axis=1))
        grads['mla_w_uq'] = _to_slabs(
            grads['mla_w_uq'].reshape(rq, h_mla, MLA_QK_PAD)[:, :, :MLA_QK].reshape(rq, h_mla * MLA_QK))
        late = (l, _reduce_begin([grads[n] for n in late_names]))
    shard_grads[0].update(zip(late_names, _reduce_end(late[1], _run_job(late[1], "rs_exchange_chips"))))

    gg = _pack_gains({n: jnp.stack(gain_grads[n]) for n in GAIN_NAMES})
    gg = _sum_blocks(_allgather_small(gg, "allgather_gain_grads"), N_DEV, "sum_gain_grads")

    grad_w, delta_w, new_m, new_v = {}, {}, {}, {}
    dg, mg, vg = _adamw(_pack_gains(ws), gg, _pack_gains(ms), _pack_gains(vs), "adamw_gains")
    for dst, slab in ((grad_w, gg), (delta_w, dg), (new_m, mg), (new_v, vg)):
        dst.update(_unpack_gains(slab, gain_shapes))
    for n in MATRIX_NAMES:
        shape = ws[n].shape
        g = jnp.stack([shard_grads[l][n] for l in range(depth)])
        flat = lambda a: a.reshape(shape[0] * shape[1], shape[2])
        d_, m_, v_ = _adamw(flat(ws[n]), flat(g), flat(ms[n]), flat(vs[n]), "adamw_" + n)
        grad_w[n], delta_w[n], new_m[n], new_v[n] = g, d_.reshape(shape), m_.reshape(shape), v_.reshape(shape)

    loss = lax.psum(loss[0, 0], ("x", "y", "c"))
    return (loss, dx[None], *[grad_w[n] for n in WEIGHT_NAMES], *[delta_w[n] for n in WEIGHT_NAMES],
            *[new_m[n] for n in WEIGHT_NAMES], *[new_v[n] for n in WEIGHT_NAMES])
```

```python
import functools
import math

import jax
import jax.numpy as jnp
from jax import lax
from jax.experimental import pallas as pl
from jax.experimental.pallas import tpu as pltpu

F32 = jnp.float32
BF16 = jnp.bfloat16

HEAD_DIM = 128
MLA_NOPE = 128
MLA_ROPE = 64
MLA_V = 128
MLA_QK = MLA_NOPE + MLA_ROPE
MLA_QK_PAD = 256
DIL_PATTERNS = ((128, 1), (512, 4), (2048, 16))
DIL_BLK = 128
PARTIAL_ROPE = HEAD_DIM // 4
ROPE_THETA = 500000.0
RMS_EPS = 1e-6
NEG = -1e30
ADAM_LR, ADAM_B1, ADAM_B2, ADAM_EPS, ADAM_WD, ADAM_STEP = 0.001, 0.9, 0.999, 1e-08, 0.01, 10

N_CHIPS = 4
N_DEV = 8
LANES = 128
PACK_W = 1024
VMEM_LIMIT = 56 * 1024 * 1024
MESH = pl.DeviceIdType.MESH

WEIGHT_NAMES = ('ffn1_pre_g', 'ffn1_post_g', 'ffn1_w_gate', 'ffn1_w_up', 'ffn1_w_down', 'mix_pre_g', 'mix_post_g',
                'w_in', 'mla_q_norm_g', 'mla_w_uq', 'mla_kv_norm_g', 'mla_w_ukv', 'w_o', 'ffn2_pre_g', 'ffn2_post_g',
                'ffn2_w_gate', 'ffn2_w_up', 'ffn2_w_down')
GAIN_NAMES = tuple(n for n in WEIGHT_NAMES if n.endswith('_g'))
MATRIX_NAMES = tuple(n for n in WEIGHT_NAMES if not n.endswith('_g'))


def _tile(dim, pref, mult=LANES):
    if dim <= pref:
        return dim
    t = (pref // mult) * mult
    while t >= mult:
        if dim % t == 0:
            return t
        t -= mult
    return dim


def _params(*sem):
    return pltpu.CompilerParams(dimension_semantics=sem, vmem_limit_bytes=VMEM_LIMIT)


def _dot(a, b, dims):
    return lax.dot_general(a.astype(BF16), b.astype(BF16), (dims, ((), ())), preferred_element_type=F32)


_NN = ((1,), (0,))
_NT = ((1,), (1,))
_TN = ((0,), (0,))


def _mm(pairs, mode, out_dtype, name, tm=1024, tn=512, tk=512, b_slabs=False, out_slabs=1):
    a0, b0 = pairs[0]
    if mode == 'nn':
        m, k = a0.shape
        n = b0.shape[0] * b0.shape[2] if b_slabs else b0.shape[1]
        tn = b0.shape[2] if b_slabs else tn
    elif mode == 'nt':
        m, k = a0.shape
        n = b0.shape[1] if b_slabs else b0.shape[0]
        tk = b0.shape[2] if b_slabs else tk
    else:
        (k, m), n = a0.shape, b0.shape[1]
        tn = n // out_slabs if out_slabs > 1 else tn
    tm, tn, tk = _tile(m, tm), _tile(n, tn), _tile(k, tk)
    nk = k // tk
    dims = {'nn': _NN, 'nt': _NT, 'tn': _TN}[mode]
    if mode == 'tn':
        a_spec = pl.BlockSpec((tk, tm), lambda i, j, kk: (kk, i))
    else:
        a_spec = pl.BlockSpec((tm, tk), lambda i, j, kk: (i, kk))
    if mode == 'nt':
        b_spec = (pl.BlockSpec((None, tn, tk), lambda i, j, kk: (kk, j, 0)) if b_slabs
                  else pl.BlockSpec((tn, tk), lambda i, j, kk: (j, kk)))
    else:
        b_spec = (pl.BlockSpec((None, tk, tn), lambda i, j, kk: (j, kk, 0)) if b_slabs
                  else pl.BlockSpec((tk, tn), lambda i, j, kk: (kk, j)))
    if out_slabs > 1:
        out_spec = pl.BlockSpec((None, tm, tn), lambda i, j, kk: (j, i, 0))
        out_shape = jax.ShapeDtypeStruct((out_slabs, m, tn), out_dtype)
    else:
        out_spec = pl.BlockSpec((tm, tn), lambda i, j, kk: (i, j))
        out_shape = jax.ShapeDtypeStruct((m, n), out_dtype)
    n_pairs = len(pairs)

    def body(*refs):
        o_ref = refs[2 * n_pairs]

        def partial_sum():
            s = _dot(refs[0][...], refs[1][...], dims)
            for p in range(1, n_pairs):
                s = s + _dot(refs[2 * p][...], refs[2 * p + 1][...], dims)
            return s

        if nk == 1:
            o_ref[...] = partial_sum().astype(out_dtype)
        else:
            acc = refs[2 * n_pairs + 1]
            kk = pl.program_id(2)

            @pl.when(kk == 0)
            def _():
                acc[...] = jnp.zeros_like(acc)

            acc[...] += partial_sum()

            @pl.when(kk == nk - 1)
            def _():
                o_ref[...] = acc[...].astype(out_dtype)

    return pl.pallas_call(
        body, name=name, grid=(m // tm, n // tn, nk),
        in_specs=[a_spec, b_spec] * n_pairs,
        out_specs=out_spec, out_shape=out_shape,
        scratch_shapes=[pltpu.VMEM((tm, tn), F32)] if nk > 1 else [],
        compiler_params=_params("parallel", "parallel", "arbitrary"),
    )(*[t for pair in pairs for t in pair])


def _ffn_up(h, wg, wu, name, job=None):
    m, d = h.shape
    n_slabs, _, fs = wg.shape
    tm = _tile(m, 512)

    def body(h_ref, wg_ref, wu_ref, g_ref, u_ref, a_ref):
        g = _dot(h_ref[...], wg_ref[...], _NN)
        u = _dot(h_ref[...], wu_ref[...], _NN)
        g_ref[...] = g.astype(BF16)
        u_ref[...] = u.astype(BF16)
        a_ref[...] = (g * jax.nn.sigmoid(g) * u).astype(BF16)

    w_spec = pl.BlockSpec((None, d, fs), lambda j, i: (j, 0, 0))
    o_spec = pl.BlockSpec((tm, fs), lambda j, i: (i, j))
    shape = jax.ShapeDtypeStruct((m, n_slabs * fs), BF16)
    return _carry_call(job, body, name, (n_slabs, m // tm),
                       [pl.BlockSpec((tm, d), lambda j, i: (i, 0)), w_spec, w_spec], [o_spec] * 3, [shape] * 3, [],
                       ("parallel", "arbitrary"), (h, wg, wu))


def _ffn_bwd_act(dz, wd, g, u, name):
    m, d = dz.shape
    f = wd.shape[0]
    tm, tn = _tile(m, 1024), _tile(f, 512)

    def body(dz_ref, wd_ref, g_ref, u_ref, dg_ref, du_ref):
        da = _dot(dz_ref[...], wd_ref[...], _NT)
        gg = g_ref[...].astype(F32)
        uu = u_ref[...].astype(F32)
        sg = jax.nn.sigmoid(gg)
        du_ref[...] = (da * (gg * sg)).astype(BF16)
        dg_ref[...] = (da * uu * (sg * (1.0 + gg * (1.0 - sg)))).astype(BF16)

    t_spec = pl.BlockSpec((tm, tn), lambda i, j: (i, j))
    shape = jax.ShapeDtypeStruct((m, f), BF16)
    return pl.pallas_call(
        body, name=name, grid=(m // tm, f // tn),
        in_specs=[pl.BlockSpec((tm, d), lambda i, j: (i, 0)), pl.BlockSpec((tn, d), lambda i, j: (j, 0)), t_spec, t_spec],
        out_specs=[t_spec] * 2, out_shape=[shape] * 2,
        compiler_params=_params("parallel", "arbitrary"),
    )(dz, wd, g, u)


def _rms(x, g):
    r = lax.rsqrt(jnp.mean(x * x, axis=-1, keepdims=True) + RMS_EPS)
    return x * r * g


def _rms_fwd(x, g, name, col_block=0, out_dtype=BF16):
    t = x.shape[0]
    w = g.shape[-1]
    tr = _tile(t, 256)

    def body(x_ref, g_ref, o_ref):
        o_ref[...] = _rms(x_ref[...].astype(F32), g_ref[...]).astype(out_dtype)

    return pl.pallas_call(
        body, name=name, grid=(t // tr,),
        in_specs=[pl.BlockSpec((tr, w), lambda i: (i, col_block)), pl.BlockSpec((1, w), lambda i: (0, 0))],
        out_specs=pl.BlockSpec((tr, w), lambda i: (i, 0)),
        out_shape=jax.ShapeDtypeStruct((t, w), out_dtype),
        compiler_params=_params("parallel"),
    )(x, g.reshape(1, w))


def _post_pre(x, z, g_post, alpha, g_next, name):
    t, w = x.shape
    tr = _tile(t, 256)

    def body(x_ref, z_ref, gp_ref, gn_ref, xo_ref, h_ref):
        xn = x_ref[...] + alpha * _rms(z_ref[...], gp_ref[...])
        xo_ref[...] = xn
        h_ref[...] = _rms(xn, gn_ref[...]).astype(BF16)

    row = pl.BlockSpec((tr, w), lambda i: (i, 0))
    vec = pl.BlockSpec((1, w), lambda i: (0, 0))
    return pl.pallas_call(
        body, name=name, grid=(t // tr,),
        in_specs=[row, row, vec, vec], out_specs=[row, row],
        out_shape=[jax.ShapeDtypeStruct((t, w), F32), jax.ShapeDtypeStruct((t, w), BF16)],
        compiler_params=_params("parallel"),
    )(x, z, g_post.reshape(1, w), g_next.reshape(1, w))


def _post_loss(x, z, g_post, alpha, target, name):
    t, w = x.shape
    tr = _tile(t, 256)

    def body(x_ref, z_ref, gp_ref, t_ref, dy_ref, loss_ref):
        err = x_ref[...] + alpha * _rms(z_ref[...], gp_ref[...]) - t_ref[...]
        dy_ref[...] = err * (1.0 / w)

        @pl.when(pl.program_id(0) == 0)
        def _():
            loss_ref[...] = jnp.zeros_like(loss_ref)

        loss_ref[...] += jnp.sum(jnp.mean(err * err, axis=-1, keepdims=True), axis=0, keepdims=True) * 0.5

    row = pl.BlockSpec((tr, w), lambda i: (i, 0))
    vec = pl.BlockSpec((1, w), lambda i: (0, 0))
    return pl.pallas_call(
        body, name=name, grid=(t // tr,),
        in_specs=[row, row, vec, row], out_specs=[row, pl.BlockSpec((1, 1), lambda i: (0, 0))],
        out_shape=[jax.ShapeDtypeStruct((t, w), F32), jax.ShapeDtypeStruct((1, 1), F32)],
        compiler_params=_params("arbitrary"),
    )(x, z, g_post.reshape(1, w), target)


def _rms_bwd(x, g, dy, alpha, name, res=None, col_block=0, out_dtype=F32):
    t = x.shape[0]
    w = g.shape[-1]
    tr = _tile(t, 256)
    has_res = res is not None

    def body(*refs):
        x_ref, g_ref, dy_ref = refs[:3]
        dx_ref, dg_ref = refs[-2:]
        xx = x_ref[...].astype(F32)
        dyy = dy_ref[...].astype(F32) * alpha
        r = lax.rsqrt(jnp.mean(xx * xx, axis=-1, keepdims=True) + RMS_EPS)
        xh = xx * r
        gy = dyy * g_ref[...]
        dx = r * (gy - xh * jnp.mean(gy * xh, axis=-1, keepdims=True))
        if has_res:
            dx = dx + refs[3][...]
        dx_ref[...] = dx.astype(out_dtype)

        @pl.when(pl.program_id(0) == 0)
        def _():
            dg_ref[...] = jnp.zeros_like(dg_ref)

        dg_ref[...] += jnp.sum(dyy * xh, axis=0, keepdims=True)

    row = pl.BlockSpec((tr, w), lambda i: (i, 0))
    vec = pl.BlockSpec((1, w), lambda i: (0, 0))
    ins = [x, g.reshape(1, w), dy] + ([res] if has_res else [])
    dx, dg = pl.pallas_call(
        body, name=name, grid=(t // tr,),
        in_specs=[pl.BlockSpec((tr, w), lambda i: (i, col_block)), vec, row] + ([row] if has_res else []),
        out_specs=[row, vec],
        out_shape=[jax.ShapeDtypeStruct((t, w), out_dtype), jax.ShapeDtypeStruct((1, w), F32)],
        compiler_params=_params("arbitrary"),
    )(*ins)
    return dx, dg.reshape(w)


def _rot_tables(cos, sin, period, start, fwd):
    t, s = cos.shape
    one = jnp.ones((t, start), F32)
    tail_w = period - start - 2 * s
    tail = jnp.ones((t, tail_w), F32) if start == 0 else jnp.zeros((t, tail_w), F32)
    z = lambda w: jnp.zeros((t, w), F32)
    c = jnp.concatenate([one, cos, cos, tail], axis=1)
    sm = jnp.concatenate([z(start + s), sin, z(tail_w)], axis=1)
    sp = jnp.concatenate([z(start), -sin, z(s + tail_w)], axis=1)
    if fwd:
        return c, sm, sp
    return c, jnp.roll(sp, s, axis=1), jnp.roll(sm, -s, axis=1)


def _rot_apply(x, c, sm, sp, shift):
    w = x.shape[-1]
    return x * c + pltpu.roll(x, shift, 1) * sm + pltpu.roll(x, w - shift, 1) * sp


def _rot(xs, tables, shift, name, col_block=0, width=None, out_dtype=BF16):
    t = xs[0].shape[0]
    w = width if width is not None else xs[0].shape[1]
    tr = _tile(t, 256)
    n_in = len(xs)
    period = tables[0].shape[1] if tables is not None else w

    def body(*refs):
        o_ref = refs[-1]
        for hh in range(w // period):
            sl = slice(hh * period, (hh + 1) * period)
            v = refs[0][:, sl].astype(F32)
            for p in range(1, n_in):
                v = v + refs[p][:, sl].astype(F32)
            if tables is not None:
                c_ref, sm_ref, sp_ref = refs[n_in:n_in + 3]
                v = _rot_apply(v, c_ref[...], sm_ref[...], sp_ref[...], shift)
            o_ref[:, sl] = v.astype(out_dtype)

    tab = pl.BlockSpec((tr, period), lambda i: (i, 0))
    return pl.pallas_call(
        body, name=name, grid=(t // tr,),
        in_specs=[pl.BlockSpec((tr, w), lambda i: (i, col_block))] * n_in + ([tab] * 3 if tables is not None else []),
        out_specs=pl.BlockSpec((tr, w), lambda i: (i, 0)),
        out_shape=jax.ShapeDtypeStruct((t, w), out_dtype),
        compiler_params=_params("parallel"),
    )(*xs, *(tables if tables is not None else ()))


def _mla_kfull(kv, proj, kr_block, tables, heads, name):
    t = kv.shape[0]
    tr = _tile(t, 256)

    def body(kv_ref, kr_ref, c_ref, sm_ref, sp_ref, o_ref):
        kr = _rot_apply(kr_ref[...].astype(F32), c_ref[...], sm_ref[...], sp_ref[...], MLA_ROPE // 2).astype(BF16)
        for hh in range(heads):
            o_ref[:, hh * 256:hh * 256 + 128] = kv_ref[:, hh * 256:hh * 256 + 128]
            o_ref[:, hh * 256 + 128:(hh + 1) * 256] = kr

    tab = pl.BlockSpec((tr, LANES), lambda i: (i, 0))
    full = pl.BlockSpec((tr, heads * 256), lambda i: (i, 0))
    return pl.pallas_call(
        body, name=name, grid=(t // tr,),
        in_specs=[full, pl.BlockSpec((tr, LANES), lambda i: (i, kr_block)), tab, tab, tab],
        out_specs=full, out_shape=jax.ShapeDtypeStruct((t, heads * 256), BF16),
        compiler_params=_params("parallel"),
    )(kv, proj, *tables)


def _mla_dkv(dk, dv, tables, heads, name):
    t = dk.shape[0]
    tr = _tile(t, 256)

    def body(dk_ref, dv_ref, c_ref, sm_ref, sp_ref, dkv_ref, dkr_ref):
        acc = jnp.zeros((tr, LANES), F32)
        for hh in range(heads):
            dkv_ref[:, hh * 256:hh * 256 + 128] = dk_ref[:, hh * 256:hh * 256 + 128].astype(BF16)
            dkv_ref[:, hh * 256 + 128:(hh + 1) * 256] = dv_ref[:, hh * 128:(hh + 1) * 128].astype(BF16)
            acc = acc + dk_ref[:, hh * 256 + 128:(hh + 1) * 256]
        dkr_ref[...] = _rot_apply(acc, c_ref[...], sm_ref[...], sp_ref[...], MLA_ROPE // 2).astype(BF16)

    tab = pl.BlockSpec((tr, LANES), lambda i: (i, 0))
    full = pl.BlockSpec((tr, heads * 256), lambda i: (i, 0))
    return pl.pallas_call(
        body, name=name, grid=(t // tr,),
        in_specs=[full, pl.BlockSpec((tr, heads * 128), lambda i: (i, 0)), tab, tab, tab],
        out_specs=[full, tab],
        out_shape=[jax.ShapeDtypeStruct((t, heads * 256), BF16), jax.ShapeDtypeStruct((t, LANES), BF16)],
        compiler_params=_params("parallel"),
    )(dk, dv, *tables)


def _delta(do, o, name):
    t, w = do.shape
    tr = _tile(t, 256)

    def body(do_ref, o_ref, d_ref):
        for hh in range(w // HEAD_DIM):
            sl = slice(hh * HEAD_DIM, (hh + 1) * HEAD_DIM)
            s = jnp.sum(do_ref[:, sl].astype(F32) * o_ref[:, sl].astype(F32), axis=1, keepdims=True)
            d_ref[:, sl] = jnp.broadcast_to(s, (tr, HEAD_DIM))

    row = pl.BlockSpec((tr, w), lambda i: (i, 0))
    return pl.pallas_call(
        body, name=name, grid=(t // tr,), in_specs=[row, row], out_specs=row,
        out_shape=jax.ShapeDtypeStruct((t, w), F32), compiler_params=_params("parallel"),
    )(do, o)


def _causal_mask(s, row0, col0):
    qi = row0 + lax.broadcasted_iota(jnp.int32, s.shape, 0)
    kj = col0 + lax.broadcasted_iota(jnp.int32, s.shape, 1)
    return jnp.where(kj <= qi, s, NEG)


def _head_group(heads):
    return 2 if heads % 2 == 0 else 1


def _v_specs(rows, hp, row_block):
    return [pl.BlockSpec((rows, MLA_V), functools.partial(lambda h, i, hh: (row_block(i), 2 * (h * hp + hh) + 1), hh=hh))
            for hh in range(hp)]


def _flash_fwd(q, k, kv, heads, name, job=None):
    t = q.shape[0]
    tq = _tile(t, 512)
    hp = _head_group(heads)
    nrep = tq // HEAD_DIM
    scale = 1.0 / math.sqrt(MLA_QK)

    def body(q_ref, k_ref, *rest):
        v_refs = rest[:hp]
        o_ref, lse_ref, m_sc, acc_sc = rest[hp:]
        i = pl.program_id(1)
        m_sc[...] = jnp.full_like(m_sc, NEG)
        acc_sc[...] = jnp.zeros_like(acc_sc)
        ones = jnp.ones((tq, HEAD_DIM), BF16)

        def step(j, masked):
            rows = pl.ds(pl.multiple_of(j * tq, tq), tq)
            for hh in range(hp):
                hs = slice(hh * 256, (hh + 1) * 256)
                s = _dot(q_ref[:, hs], k_ref[rows, hs], _NT) * scale
                if masked:
                    s = _causal_mask(s, 0, 0)
                m_prev = m_sc[hh]
                m_new = jnp.maximum(m_prev, jnp.broadcast_to(jnp.max(s, axis=1, keepdims=True), (tq, HEAD_DIM)))
                a = jnp.exp(m_prev - m_new)
                p = jnp.exp(s - jnp.tile(m_new, (1, nrep)))
                pv = _dot(p, jnp.concatenate([v_refs[hh][rows, :], ones], axis=1), _NN)
                acc_sc[hh] = jnp.tile(a, (1, 2)) * acc_sc[hh] + pv
                m_sc[hh] = m_new

        def loop_body(j, carry):
            step(j, False)
            return carry

        lax.fori_loop(0, i, loop_body, 0)
        step(i, True)
        for hh in range(hp):
            l = acc_sc[hh][:, HEAD_DIM:]
            o_ref[:, hh * MLA_V:(hh + 1) * MLA_V] = (acc_sc[hh][:, :HEAD_DIM] / l).astype(BF16)
            lse_ref[:, hh * HEAD_DIM:(hh + 1) * HEAD_DIM] = m_sc[hh] + jnp.log(l)

    stat = pl.BlockSpec((tq, hp * HEAD_DIM), lambda h, i: (i, h))
    return _carry_call(
        job, body, name, (heads // hp, t // tq),
        [pl.BlockSpec((tq, hp * 256), lambda h, i: (i, h)), pl.BlockSpec((t, hp * 256), lambda h, i: (0, h))]
        + _v_specs(t, hp, lambda i: 0), [stat, stat],
        [jax.ShapeDtypeStruct((t, heads * MLA_V), BF16), jax.ShapeDtypeStruct((t, heads * HEAD_DIM), F32)],
        [pltpu.VMEM((hp, tq, HEAD_DIM), F32), pltpu.VMEM((hp, tq, 2 * HEAD_DIM), F32)],
        ("parallel", "arbitrary"), (q, k, *([kv] * hp)))


def _flash_bwd_dq(q, k, kv, do, lse, delta, heads, name, job=None):
    t = q.shape[0]
    tq = _tile(t, 512)
    hp = _head_group(heads)
    nrep = tq // HEAD_DIM
    scale = 1.0 / math.sqrt(MLA_QK)

    def body(q_ref, k_ref, *rest):
        v_refs = rest[:hp]
        do_ref, lse_ref, d_ref, dq_ref = rest[hp:]
        i = pl.program_id(1)
        dq_ref[...] = jnp.zeros_like(dq_ref)

        def step(j, masked):
            rows = pl.ds(pl.multiple_of(j * tq, tq), tq)
            for hh in range(hp):
                hs = slice(hh * 256, (hh + 1) * 256)
                st = slice(hh * HEAD_DIM, (hh + 1) * HEAD_DIM)
                kk = k_ref[rows, hs]
                s = _dot(q_ref[:, hs], kk, _NT) * scale
                if masked:
                    s = _causal_mask(s, 0, 0)
                p = jnp.exp(s - jnp.tile(lse_ref[:, st], (1, nrep)))
                dp = _dot(do_ref[:, st], v_refs[hh][rows, :], _NT)
                ds = p * (dp - jnp.tile(d_ref[:, st], (1, nrep))) * scale
                dq_ref[:, hs] += _dot(ds, kk, _NN)

        def loop_body(j, carry):
            step(j, False)
            return carry

        lax.fori_loop(0, i, loop_body, 0)
        step(i, True)

    stat = pl.BlockSpec((tq, hp * HEAD_DIM), lambda h, i: (i, h))
    qs = pl.BlockSpec((tq, hp * 256), lambda h, i: (i, h))
    return _carry_call(
        job, body, name, (heads // hp, t // tq),
        [qs, pl.BlockSpec((t, hp * 256), lambda h, i: (0, h))] + _v_specs(t, hp, lambda i: 0) + [stat, stat, stat],
        [qs], [jax.ShapeDtypeStruct((t, heads * 256), F32)], [], ("parallel", "arbitrary"),
        (q, k, *([kv] * hp), do, lse, delta))


def _stat_rows(stat, heads):
    return jnp.transpose(stat[:, ::HEAD_DIM]).reshape(heads, 1, stat.shape[0])


def _flash_bwd_dkv(q, k, kv, do, lse_t, delta_t, heads, name, job=None):
    t = q.shape[0]
    tq = _tile(t, 512)
    nq = t // tq
    hp = _head_group(heads)
    scale = 1.0 / math.sqrt(MLA_QK)

    def body(q_ref, k_ref, *rest):
        v_refs = rest[:hp]
        do_ref, lse_ref, d_ref, dk_ref, dv_ref = rest[hp:]
        j = pl.program_id(1)
        dk_ref[...] = jnp.zeros_like(dk_ref)
        dv_ref[...] = jnp.zeros_like(dv_ref)

        def step(i, masked):
            rows = pl.ds(pl.multiple_of(i * tq, tq), tq)
            for hh in range(hp):
                hs = slice(hh * 256, (hh + 1) * 256)
                st = slice(hh * HEAD_DIM, (hh + 1) * HEAD_DIM)
                qq = q_ref[rows, hs]
                dd = do_ref[rows, st]
                s = _dot(k_ref[:, hs], qq, _NT) * scale
                if masked:
                    kj = lax.broadcasted_iota(jnp.int32, s.shape, 0)
                    qi = lax.broadcasted_iota(jnp.int32, s.shape, 1)
                    s = jnp.where(kj <= qi, s, NEG)
                p = jnp.exp(s - lse_ref[hh, :, rows])
                dv_ref[:, st] += _dot(p, dd, _NN)
                dp = _dot(v_refs[hh][...], dd, _NT)
                ds = p * (dp - d_ref[hh, :, rows]) * scale
                dk_ref[:, hs] += _dot(ds, qq, _NN)

        def loop_body(i, carry):
            step(i, False)
            return carry

        step(j, True)
        lax.fori_loop(j + 1, nq, loop_body, 0)

    ks = pl.BlockSpec((tq, hp * 256), lambda h, j: (j, h))
    row = pl.BlockSpec((hp, 1, t), lambda h, j: (h, 0, 0))
    return _carry_call(
        job, body, name, (heads // hp, nq),
        [pl.BlockSpec((t, hp * 256), lambda h, j: (0, h)), ks] + _v_specs(tq, hp, lambda j: j)
        + [pl.BlockSpec((t, hp * MLA_V), lambda h, j: (0, h)), row, row],
        [ks, pl.BlockSpec((tq, hp * MLA_V), lambda h, j: (j, h))],
        [jax.ShapeDtypeStruct((t, heads * 256), F32), jax.ShapeDtypeStruct((t, heads * MLA_V), F32)], [],
        ("parallel", "arbitrary"), (q, k, *([kv] * hp), do, lse_t, delta_t))


def _dil_view(a, d):
    t, w = a.shape
    return a.reshape(t // d, d * w)


def _dil_band(first_block):
    qi = lax.broadcasted_iota(jnp.int32, (DIL_BLK, 2 * DIL_BLK), 0)
    kj = lax.broadcasted_iota(jnp.int32, (DIL_BLK, 2 * DIL_BLK), 1)
    prev = jnp.logical_and(jnp.logical_and(kj < DIL_BLK, kj >= qi), jnp.logical_not(first_block))
    return jnp.logical_or(prev, jnp.logical_and(kj >= DIL_BLK, kj - DIL_BLK <= qi))


def _dil_fwd(q, k, v, d, name):
    t, w = q.shape
    nb = t // d // DIL_BLK
    scale = 1.0 / math.sqrt(HEAD_DIM)

    def body(q_ref, kp_ref, kc_ref, vp_ref, vc_ref, o_ref, lse_ref):
        mask = _dil_band(pl.program_id(1) == 0)
        ones = jnp.ones((2 * DIL_BLK, HEAD_DIM), BF16)
        for hh in range(w // HEAD_DIM):
            sl = slice(hh * HEAD_DIM, (hh + 1) * HEAD_DIM)
            kk = jnp.concatenate([kp_ref[:, sl], kc_ref[:, sl]], axis=0)
            vv = jnp.concatenate([jnp.concatenate([vp_ref[:, sl], vc_ref[:, sl]], axis=0), ones], axis=1)
            s = jnp.where(mask, _dot(q_ref[:, sl], kk, _NT) * scale, NEG)
            m = jnp.broadcast_to(jnp.max(s, axis=1, keepdims=True), (DIL_BLK, HEAD_DIM))
            pv = _dot(jnp.exp(s - jnp.tile(m, (1, 2))), vv, _NN)
            l = pv[:, HEAD_DIM:]
            o_ref[:, sl] = (pv[:, :HEAD_DIM] / l).astype(BF16)
            lse_ref[:, sl] = m + jnp.log(l)

    cur = pl.BlockSpec((DIL_BLK, w), lambda r, n: (n, r))
    prev = pl.BlockSpec((DIL_BLK, w), lambda r, n: (jnp.maximum(n - 1, 0), r))
    o, lse = pl.pallas_call(
        body, name=name, grid=(d, nb),
        in_specs=[cur, prev, cur, prev, cur], out_specs=[cur, cur],
        out_shape=[jax.ShapeDtypeStruct((t // d, d * w), BF16), jax.ShapeDtypeStruct((t // d, d * w), F32)],
        compiler_params=_params("parallel", "arbitrary"),
    )(_dil_view(q, d), _dil_view(k, d), _dil_view(k, d), _dil_view(v, d), _dil_view(v, d))
    return o.reshape(t, w), lse.reshape(t, w)


def _dil_combine(os_, lses, name):
    t, w = os_[0].shape
    tr = _tile(t, 256)
    n = len(os_)

    def body(*refs):
        o_ref, m_ref = refs[-2:]
        ls = [refs[n + p][...] for p in range(n)]
        m = functools.reduce(jnp.maximum, ls)
        es = [jnp.exp(l - m) for l in ls]
        den = functools.reduce(jnp.add, es)
        acc = es[0] / den * refs[0][...].astype(F32)
        for p in range(1, n):
            acc = acc + es[p] / den * refs[p][...].astype(F32)
        o_ref[...] = acc.astype(BF16)
        m_ref[...] = m + jnp.log(den)

    row = pl.BlockSpec((tr, w), lambda i: (i, 0))
    return pl.pallas_call(
        body, name=name, grid=(t // tr,), in_specs=[row] * (2 * n), out_specs=[row, row],
        out_shape=[jax.ShapeDtypeStruct((t, w), BF16), jax.ShapeDtypeStruct((t, w), F32)],
        compiler_params=_params("parallel"),
    )(*os_, *lses)


def _dil_bwd_dq(q, k, v, do, mt, delta, d, name):
    t, w = q.shape
    nb = t // d // DIL_BLK
    scale = 1.0 / math.sqrt(HEAD_DIM)

    def body(q_ref, kp_ref, kc_ref, vp_ref, vc_ref, do_ref, mt_ref, d_ref, dq_ref):
        mask = _dil_band(pl.program_id(1) == 0)
        for hh in range(w // HEAD_DIM):
            sl = slice(hh * HEAD_DIM, (hh + 1) * HEAD_DIM)
            kk = jnp.concatenate([kp_ref[:, sl], kc_ref[:, sl]], axis=0)
            vv = jnp.concatenate([vp_ref[:, sl], vc_ref[:, sl]], axis=0)
            s = jnp.where(mask, _dot(q_ref[:, sl], kk, _NT) * scale, NEG)
            p = jnp.exp(s - jnp.tile(mt_ref[:, sl], (1, 2)))
            ds = p * (_dot(do_ref[:, sl], vv, _NT) - jnp.tile(d_ref[:, sl], (1, 2))) * scale
            dq_ref[:, sl] = _dot(ds, kk, _NN).astype(BF16)

    cur = pl.BlockSpec((DIL_BLK, w), lambda r, n: (n, r))
    prev = pl.BlockSpec((DIL_BLK, w), lambda r, n: (jnp.maximum(n - 1, 0), r))
    kv_, vv_ = _dil_view(k, d), _dil_view(v, d)
    dq = pl.pallas_call(
        body, name=name, grid=(d, nb),
        in_specs=[cur, prev, cur, prev, cur, cur, cur, cur], out_specs=cur,
        out_shape=jax.ShapeDtypeStruct((t // d, d * w), BF16),
        compiler_params=_params("parallel", "arbitrary"),
    )(_dil_view(q, d), kv_, kv_, vv_, vv_, _dil_view(do, d), _dil_view(mt, d), _dil_view(delta, d))
    return dq.reshape(t, w)


def _dil_bwd_dkv(q, k, v, do, mt, delta, d, name):
    t, w = q.shape
    nb = t // d // DIL_BLK
    scale = 1.0 / math.sqrt(HEAD_DIM)

    def body(k_ref, v_ref, qs_ref, qn_ref, dos_ref, don_ref, ms_ref, mn_ref, ds_ref, dn_ref, dk_ref, dv_ref):
        last = pl.program_id(1) == nb - 1
        qi = lax.broadcasted_iota(jnp.int32, (2 * DIL_BLK, DIL_BLK), 0)
        kj = lax.broadcasted_iota(jnp.int32, (2 * DIL_BLK, DIL_BLK), 1)
        nxt = jnp.logical_and(jnp.logical_and(qi >= DIL_BLK, kj >= qi - DIL_BLK), jnp.logical_not(last))
        mask = jnp.logical_or(jnp.logical_and(qi < DIL_BLK, kj <= qi), nxt)
        for hh in range(w // HEAD_DIM):
            sl = slice(hh * HEAD_DIM, (hh + 1) * HEAD_DIM)
            qq = jnp.concatenate([qs_ref[:, sl], qn_ref[:, sl]], axis=0)
            dd = jnp.concatenate([dos_ref[:, sl], don_ref[:, sl]], axis=0)
            mm = jnp.concatenate([ms_ref[:, sl], mn_ref[:, sl]], axis=0)
            de = jnp.concatenate([ds_ref[:, sl], dn_ref[:, sl]], axis=0)
            s = jnp.where(mask, _dot(qq, k_ref[:, sl], _NT) * scale, NEG)
            p = jnp.exp(s - mm)
            dv_ref[:, sl] = _dot(p, dd, _TN).astype(BF16)
            dsv = p * (_dot(dd, v_ref[:, sl], _NT) - de) * scale
            dk_ref[:, sl] = _dot(dsv, qq, _TN).astype(BF16)

    cur = pl.BlockSpec((DIL_BLK, w), lambda r, n: (n, r))
    nxt = pl.BlockSpec((DIL_BLK, w), lambda r, n: (jnp.minimum(n + 1, nb - 1), r))
    shape = jax.ShapeDtypeStruct((t // d, d * w), BF16)
    qv, dov, mv, dv_ = _dil_view(q, d), _dil_view(do, d), _dil_view(mt, d), _dil_view(delta, d)
    dk, dv = pl.pallas_call(
        body, name=name, grid=(d, nb),
        in_specs=[cur, cur, cur, nxt, cur, nxt, cur, nxt, cur, nxt], out_specs=[cur, cur], out_shape=[shape, shape],
        compiler_params=_params("parallel", "arbitrary"),
    )(_dil_view(k, d), _dil_view(v, d), qv, qv, dov, dov, mv, mv, dv_, dv_)
    return dk.reshape(t, w), dv.reshape(t, w)


def _adamw(w, g, m, v, name):
    r, c = w.shape
    tr = _tile(r, 256, 8)
    c1 = 1.0 - ADAM_B1 ** ADAM_STEP
    c2 = 1.0 - ADAM_B2 ** ADAM_STEP

    def body(w_ref, g_ref, m_ref, v_ref, d_ref, mo_ref, vo_ref):
        gg = g_ref[...]
        mn = ADAM_B1 * m_ref[...] + (1.0 - ADAM_B1) * gg
        vn = ADAM_B2 * v_ref[...] + (1.0 - ADAM_B2) * (gg * gg)
        mo_ref[...] = mn
        vo_ref[...] = vn
        d_ref[...] = -ADAM_LR * ((mn / c1) / (jnp.sqrt(vn / c2) + ADAM_EPS) + ADAM_WD * w_ref[...])

    row = pl.BlockSpec((tr, c), lambda i: (i, 0))
    shape = jax.ShapeDtypeStruct((r, c), F32)
    return pl.pallas_call(
        body, name=name, grid=(r // tr,), in_specs=[row] * 4, out_specs=[row] * 3, out_shape=[shape] * 3,
        compiler_params=_params("parallel"),
    )(w, g, m, v)


def _position():
    return lax.axis_index("x"), lax.axis_index("y"), lax.axis_index("c")


def _other_chips(x, y):
    return [(1 - x, y), (x, 1 - y), (1 - x, 1 - y)]


_ANY = pl.BlockSpec(memory_space=pl.ANY)


def _half_rows(a, half):
    return pl.ds(pl.multiple_of(half * (a // 2), 16), a // 2)


class _GatherJob:
    def __init__(self, shards):
        self.inputs = list(shards)
        n = len(shards)
        self.out_shapes = [jax.ShapeDtypeStruct((N_CHIPS,) + s.shape, s.dtype) for s in shards]
        self.scratch = [pltpu.SemaphoreType.DMA((2, 3, n)), pltpu.SemaphoreType.DMA((2, 3, n)),
                        pltpu.SemaphoreType.DMA((n,))]

    def _copies(self, w_refs, out_refs, sems):
        send_sems, recv_sems, local_sems = sems
        x, y, c = _position()
        me = 2 * x + y
        n = len(self.inputs)

        def copy(level, kk, i, src, dst, to):
            return functools.partial(
                pltpu.make_async_remote_copy, src_ref=src, dst_ref=dst, send_sem=send_sems.at[level, kk, i],
                recv_sem=recv_sems.at[level, kk, i], device_id=to, device_id_type=MESH)

        mine = [functools.partial(pltpu.make_async_copy, w_refs[i], out_refs[i].at[me], local_sems.at[i]) for i in range(n)]
        sends, landed, passed = [], [], []
        for i in range(n):
            a = self.inputs[i].shape[0]
            for kk, (px, py) in enumerate(_other_chips(x, y)):
                rows = _half_rows(a, c)
                sends.append(copy(0, kk, i, w_refs[i].at[rows], out_refs[i].at[me, rows], (px, py, c)))
                got = out_refs[i].at[2 * px + py, rows]
                landed.append((copy(0, kk, i, got, got, (px, py, c)), copy(1, kk, i, got, got, (x, y, 1 - c))))
                got = out_refs[i].at[2 * px + py, _half_rows(a, 1 - c)]
                passed.append(copy(1, kk, i, got, got, (x, y, 1 - c)))
        return mine, sends, landed, passed

    def start(self, w_refs, out_refs, sems):
        mine, sends, _, _ = self._copies(w_refs, out_refs, sems)
        for make in sends + mine:
            make().start()

    def finish(self, w_refs, out_refs, sems):
        mine, sends, landed, passed = self._copies(w_refs, out_refs, sems)
        forwards = []
        for arrival, forward in landed:
            arrival().wait_recv()
            forwards.append(forward())
            forwards[-1].start()
        for make in passed:
            make().wait_recv()
        for make in sends:
            make().wait_send()
        for cp in forwards:
            cp.wait_send()
        for make in mine:
            make().wait()


class _ExchangeJob:
    def __init__(self, sums):
        self.inputs = list(sums)
        n = len(sums)
        self.out_shapes = [jax.ShapeDtypeStruct((3,) + s.shape[1:], s.dtype) for s in sums]
        self.scratch = [pltpu.SemaphoreType.DMA((n, 3)), pltpu.SemaphoreType.DMA((n, 3))]

    def _copies(self, s_refs, out_refs, sems):
        send_sems, recv_sems = sems
        x, y, c = _position()
        return [pltpu.make_async_remote_copy(
            src_ref=s_refs[i].at[2 * px + py], dst_ref=out_refs[i].at[kk], send_sem=send_sems.at[i, kk],
            recv_sem=recv_sems.at[i, kk], device_id=(px, py, c), device_id_type=MESH)
            for i in range(len(self.inputs)) for kk, (px, py) in enumerate(_other_chips(x, y))]

    def start(self, s_refs, out_refs, sems):
        for cp in self._copies(s_refs, out_refs, sems):
            cp.start()

    def finish(self, s_refs, out_refs, sems):
        cps = self._copies(s_refs, out_refs, sems)
        for cp in cps:
            cp.wait_recv()
        for cp in cps:
            cp.wait_send()


def _run_job(job, name):
    ni, no = len(job.inputs), len(job.out_shapes)

    def body(*refs):
        job.start(refs[:ni], refs[ni:ni + no], refs[ni + no:])
        job.finish(refs[:ni], refs[ni:ni + no], refs[ni + no:])

    return pl.pallas_call(body, name=name, in_specs=[_ANY] * ni, out_specs=[_ANY] * no, out_shape=job.out_shapes,
                          scratch_shapes=job.scratch)(*job.inputs)


def _carry(job, body, n_in, n_out, grid):
    if job is None:
        return body
    ni, no, ns = len(job.inputs), len(job.out_shapes), len(job.scratch)

    def carried(*refs):
        own_in, job_in = refs[:n_in], refs[n_in:n_in + ni]
        own_out = refs[n_in + ni:n_in + ni + n_out]
        job_out = refs[n_in + ni + n_out:n_in + ni + n_out + no]
        scratch = refs[n_in + ni + n_out + no:]
        own_scratch, job_scratch = scratch[:len(scratch) - ns], scratch[len(scratch) - ns:]
        ids = [pl.program_id(ax) for ax in range(len(grid))]
        first = functools.reduce(jnp.logical_and, [i == 0 for i in ids])
        last = functools.reduce(jnp.logical_and, [i == g - 1 for i, g in zip(ids, grid)])

        @pl.when(first)
        def _():
            job.start(job_in, job_out, job_scratch)

        body(*own_in, *own_out, *own_scratch)

        @pl.when(last)
        def _():
            job.finish(job_in, job_out, job_scratch)

    return carried


def _carry_call(job, body, name, grid, in_specs, out_specs, out_shape, scratch_shapes, sem, args):
    n_out = len(out_shape)
    if job is None:
        outs = pl.pallas_call(body, name=name, grid=grid, in_specs=in_specs, out_specs=out_specs, out_shape=out_shape,
                              scratch_shapes=scratch_shapes, compiler_params=_params(*sem))(*args)
        return list(outs), None
    outs = pl.pallas_call(
        _carry(job, body, len(in_specs), n_out, grid), name=name, grid=grid,
        in_specs=list(in_specs) + [_ANY] * len(job.inputs), out_specs=list(out_specs) + [_ANY] * len(job.out_shapes),
        out_shape=list(out_shape) + job.out_shapes, scratch_shapes=list(scratch_shapes) + job.scratch,
        compiler_params=_params(*(["arbitrary"] * len(grid))),
    )(*args, *job.inputs)
    return list(outs[:n_out]), list(outs[n_out:])


def _swap_halves(gs, name):
    n = len(gs)

    def body(*refs):
        g_refs, out_refs = refs[:n], refs[n:2 * n]
        send_sems, recv_sems = refs[2 * n:]
        x, y, c = _position()
        cps = []
        for i in range(n):
            cp = pltpu.make_async_remote_copy(
                src_ref=g_refs[i].at[:, _half_rows(gs[i].shape[1], 1 - c)], dst_ref=out_refs[i],
                send_sem=send_sems.at[i], recv_sem=recv_sems.at[i], device_id=(x, y, 1 - c), device_id_type=MESH)
            cp.start()
            cps.append(cp)
        for cp in cps:
            cp.wait_recv()
        for cp in cps:
            cp.wait_send()

    return pl.pallas_call(
        body, name=name, in_specs=[_ANY] * n, out_specs=[_ANY] * n,
        out_shape=[jax.ShapeDtypeStruct((g.shape[0], g.shape[1] // 2, g.shape[2]), g.dtype) for g in gs],
        scratch_shapes=[pltpu.SemaphoreType.DMA((n,)), pltpu.SemaphoreType.DMA((n,))],
    )(*gs)


def _share_halves(bufs, name):
    n = len(bufs)

    def body(*refs):
        in_refs, out_refs = refs[:n], refs[n:2 * n]
        send_sems, recv_sems = refs[2 * n:]
        x, y, c = _position()
        cps = []
        for i in range(n):
            cp = pltpu.make_async_remote_copy(
                src_ref=in_refs[i].at[c], dst_ref=out_refs[i].at[c], send_sem=send_sems.at[i], recv_sem=recv_sems.at[i],
                device_id=(x, y, 1 - c), device_id_type=MESH)
            cp.start()
            cps.append(cp)
        for i in range(n):
            pltpu.make_async_remote_copy(
                src_ref=in_refs[i].at[1 - c], dst_ref=out_refs[i].at[1 - c], send_sem=send_sems.at[i],
                recv_sem=recv_sems.at[i], device_id=(x, y, 1 - c), device_id_type=MESH).wait_recv()
        for cp in cps:
            cp.wait_send()

    return pl.pallas_call(
        body, name=name, in_specs=[_ANY] * n, out_specs=[_ANY] * n,
        out_shape=[jax.ShapeDtypeStruct(b.shape, b.dtype) for b in bufs],
        input_output_aliases={i: i for i in range(n)},
        scratch_shapes=[pltpu.SemaphoreType.DMA((n,)), pltpu.SemaphoreType.DMA((n,))],
    )(*bufs)


def _add_pair(g, got, name):
    n, a, b = g.shape
    ah = a // 2
    tr = _tile(ah, 256, 16)

    def body(c_ref, a_ref, b_ref, o_ref):
        o_ref[...] = (a_ref[...].astype(F32) + b_ref[...].astype(F32)).astype(BF16)

    blk = pl.BlockSpec((None, tr, b), lambda j, i, c_ref: (j, i, 0))
    return pl.pallas_call(
        body, name=name,
        grid_spec=pltpu.PrefetchScalarGridSpec(
            num_scalar_prefetch=1, grid=(n, ah // tr),
            in_specs=[pl.BlockSpec((None, None, tr, b), lambda j, i, c_ref: (j, c_ref[0], i, 0)), blk], out_specs=blk),
        out_shape=jax.ShapeDtypeStruct((n, ah, b), BF16),
        compiler_params=_params("parallel", "parallel"),
    )(lax.axis_index("c").reshape(1).astype(jnp.int32), g.reshape(n, 2, ah, b), got)


def _add_chips(sums, got, name):
    n, ah, b = sums.shape
    tr = _tile(ah, 256, 16)

    def body(at_ref, a_ref, b_ref, o_ref):
        o_ref[...] = ((a_ref[...].astype(F32) + b_ref[0].astype(F32)) + b_ref[1].astype(F32)) + b_ref[2].astype(F32)

    at = jnp.stack([2 * lax.axis_index("x") + lax.axis_index("y"), lax.axis_index("c")]).astype(jnp.int32)
    return pl.pallas_call(
        body, name=name,
        grid_spec=pltpu.PrefetchScalarGridSpec(
            num_scalar_prefetch=1, grid=(ah // tr,),
            in_specs=[pl.BlockSpec((None, tr, b), lambda i, at_ref: (at_ref[0], i, 0)),
                      pl.BlockSpec((3, tr, b), lambda i, at_ref: (0, i, 0))],
            out_specs=pl.BlockSpec((None, tr, b), lambda i, at_ref: (at_ref[1], i, 0))),
        out_shape=jax.ShapeDtypeStruct((2, ah, b), F32),
        compiler_params=_params("parallel"),
    )(at, sums, got)


def _allgather_small(v, name):
    m_per, w = v.shape

    def body(x_ref, out_ref, send_sems, recv_sems, local_sem):
        x, y, c = _position()
        me, sibling = (x, y, c), (x, y, 1 - c)
        chips = _other_chips(x, y)

        def rows(px, py, pc):
            return out_ref.at[pl.ds(pl.multiple_of((4 * px + 2 * py + pc) * m_per, 8), m_per), :]

        def copy(kk, block, to, src=None):
            return pltpu.make_async_remote_copy(
                src_ref=rows(*block) if src is None else src, dst_ref=rows(*block), send_sem=send_sems.at[kk],
                recv_sem=recv_sems.at[kk], device_id=to, device_id_type=MESH)

        mine = pltpu.make_async_copy(x_ref, rows(*me), local_sem)
        mine.start()
        first = [copy(0, me, sibling, src=x_ref)]
        first += [copy(1 + j, me, (*chip, c), src=x_ref) for j, chip in enumerate(chips)]
        for cp in first:
            cp.start()
        passed = [copy(4 + j, (*chip, c), sibling) for j, chip in enumerate(chips)]
        for j, chip in enumerate(chips):
            copy(1 + j, (*chip, c), me).wait_recv()
            passed[j].start()
        copy(0, sibling, me).wait_recv()
        for j, chip in enumerate(chips):
            copy(4 + j, (*chip, 1 - c), me).wait_recv()
        for cp in first + passed:
            cp.wait_send()
        mine.wait()

    return pl.pallas_call(
        body, name=name,
        out_shape=jax.ShapeDtypeStruct((N_DEV * m_per, w), v.dtype),
        in_specs=[pl.BlockSpec(memory_space=pltpu.VMEM)], out_specs=pl.BlockSpec(memory_space=pltpu.VMEM),
        scratch_shapes=[pltpu.SemaphoreType.DMA((7,)), pltpu.SemaphoreType.DMA((7,)), pltpu.SemaphoreType.DMA],
    )(v)


def _sum_blocks(a, n, name):
    m_per, w = a.shape[0] // n, a.shape[1]

    def body(a_ref, o_ref):
        s = a_ref[0:m_per, :]
        for j in range(1, n):
            s = s + a_ref[j * m_per:(j + 1) * m_per, :]
        o_ref[...] = s

    return pl.pallas_call(body, name=name, out_shape=jax.ShapeDtypeStruct((m_per, w), a.dtype))(a)


def _pack_gains(d):
    flat = jnp.concatenate([d[n].reshape(1, -1) for n in GAIN_NAMES], axis=1)
    n = flat.shape[-1]
    rows = -(-n // (PACK_W * 8)) * 8
    return jnp.pad(flat, ((0, 0), (0, rows * PACK_W - n))).reshape(rows, PACK_W)


def _to_slabs(g):
    a, nb = g.shape
    return jnp.transpose(g.reshape(a, N_CHIPS, nb // N_CHIPS), (1, 0, 2))


def _from_slabs(s):
    n, a, b = s.shape
    return jnp.transpose(s, (1, 0, 2)).reshape(a, n * b)


def _unpack_gains(slab, shapes):
    flat = slab.reshape(-1)
    out, off = {}, 0
    for n in GAIN_NAMES:
        a, b = shapes[n]
        out[n] = flat[off:off + a * b].reshape(a, b)
        off += a * b
    return out


def _reduce_begin(gs):
    got = _swap_halves(gs, "rs_swap_halves")
    return _ExchangeJob([_add_pair(g, h, "rs_add_pair") for g, h in zip(gs, got)])


def _reduce_end(job, got):
    halves = [_add_chips(s, h, "rs_add_chips") for s, h in zip(job.inputs, got)]
    return [b.reshape(b.shape[1] * 2, b.shape[2]) for b in _share_halves(halves, "rs_share_halves")]


def _ffn_forward(h, wl, pre, tag, job, gathered, names):
    (g, u, a), got = _ffn_up(h, wl[pre + '_w_gate'], wl[pre + '_w_up'], "ffn_up" + tag, job)
    if got is not None:
        gathered.update(zip(names, got))
    wd = wl[pre + '_w_down']
    z = _mm([(a, wd.reshape(-1, wd.shape[2]))], 'nn', F32, "ffn_down" + tag, tn=1024, tk=wd.shape[1])
    return z, (g, u, a)


def _ffn_backward(dz, h, saved, wl, pre, grads, tag):
    g, u, a = saved
    wd = wl[pre + '_w_down']
    dg, du = _ffn_bwd_act(dz, wd.reshape(-1, wd.shape[2]), g, u, "ffn_bwd_act" + tag)
    grads[pre + '_w_down'] = _mm([(a, dz)], 'tn', BF16, "ffn_dw_down" + tag, tm=wd.shape[1], tn=1024, tk=1024).reshape(wd.shape)
    grads[pre + '_w_gate'] = _mm([(h, dg)], 'tn', BF16, "ffn_dw_gate" + tag, tk=1024, out_slabs=N_CHIPS)
    grads[pre + '_w_up'] = _mm([(h, du)], 'tn', BF16, "ffn_dw_up" + tag, tk=1024, out_slabs=N_CHIPS)
    return _mm([(dg, wl[pre + '_w_gate']), (du, wl[pre + '_w_up'])], 'nt', F32, "ffn_dh" + tag, tn=1024, b_slabs=True)


def kernel(x, positions, ffn1_pre_g, ffn1_post_g, ffn1_w_gate, ffn1_w_up, ffn1_w_down, mix_pre_g, mix_post_g, w_in, mla_q_norm_g, mla_w_uq, mla_kv_norm_g, mla_w_ukv, w_o, ffn2_pre_g, ffn2_post_g, ffn2_w_gate, ffn2_w_up, ffn2_w_down, loss_target, m_ffn1_pre_g, m_ffn1_post_g, m_ffn1_w_gate, m_ffn1_w_up, m_ffn1_w_down, m_mix_pre_g, m_mix_post_g, m_w_in, m_mla_q_norm_g, m_mla_w_uq, m_mla_kv_norm_g, m_mla_w_ukv, m_w_o, m_ffn2_pre_g, m_ffn2_post_g, m_ffn2_w_gate, m_ffn2_w_up, m_ffn2_w_down, v_ffn1_pre_g, v_ffn1_post_g, v_ffn1_w_gate, v_ffn1_w_up, v_ffn1_w_down, v_mix_pre_g, v_mix_post_g, v_w_in, v_mla_q_norm_g, v_mla_w_uq, v_mla_kv_norm_g, v_mla_w_ukv, v_w_o, v_ffn2_pre_g, v_ffn2_post_g, v_ffn2_w_gate, v_ffn2_w_up, v_ffn2_w_down):
    given = dict(locals())
    ws = {n: given[n] for n in WEIGHT_NAMES}
    ms = {n: given['m_' + n] for n in WEIGHT_NAMES}
    vs = {n: given['v_' + n] for n in WEIGHT_NAMES}

    depth = ffn1_pre_g.shape[0]
    t, d_model = x.shape[1], x.shape[2]
    rq, rkv = mla_q_norm_g.shape[1], mla_kv_norm_g.shape[1]
    h_mla = mla_w_uq.shape[2] * N_CHIPS // MLA_QK
    wd_ = (w_in.shape[2] * N_CHIPS - rq - rkv - MLA_ROPE) // 3
    assert rq == rkv and (rq + rkv) % wd_ == 0 and wd_ % HEAD_DIM == 0
    assert mla_w_ukv.shape[2] * N_CHIPS == h_mla * 256 and w_o.shape[1] * N_CHIPS == h_mla * MLA_V + wd_
    lat = rq + rkv
    in_p = lat + 3 * wd_ + LANES
    qd_block = lat // wd_
    kr_block = (lat + 3 * wd_) // LANES
    gain_shapes = {n: ws[n].shape for n in GAIN_NAMES}

    pos = positions[0].astype(F32)[:, None]

    def cos_sin(dim):
        inv = ROPE_THETA ** (-jnp.arange(0, dim, 2, dtype=F32) / dim)
        return jnp.cos(pos * inv), jnp.sin(pos * inv)

    cos_a, sin_a = cos_sin(MLA_ROPE)
    cos_p, sin_p = cos_sin(PARTIAL_ROPE)
    tab_q = [_rot_tables(cos_a, sin_a, MLA_QK_PAD, MLA_NOPE, f) for f in (True, False)]
    tab_kr = [_rot_tables(cos_a, sin_a, LANES, 0, f) for f in (True, False)]
    tab_d = [_rot_tables(cos_p, sin_p, HEAD_DIM, 0, f) for f in (True, False)]
    s_a, s_p = MLA_ROPE // 2, PARTIAL_ROPE // 2

    w16 = {n: ws[n].astype(BF16) for n in MATRIX_NAMES}
    parts = {'ffn1': [n for n in MATRIX_NAMES if n.startswith('ffn1')], 'ffn2': [n for n in MATRIX_NAMES if n.startswith('ffn2')],
             'mix': [n for n in MATRIX_NAMES if not n.startswith('ffn')]}
    parts['first'] = ['ffn1_w_gate', 'ffn1_w_up']
    parts['then'] = ['ffn1_w_down'] + parts['mix']

    def gather_job(l, part):
        return _GatherJob([w16[n][l] for n in parts[part]]) if l < depth else None

    layers = [{} for _ in range(depth)]
    layers[0].update(zip(parts['first'], _run_job(gather_job(0, 'first'), "allgather_weights")))

    xc = x[0]
    h = _rms_fwd(xc, ffn1_pre_g[0], "rms_first")
    saved = []
    dy = loss = None
    for l in range(depth):
        wl, sv = layers[l], {}
        sv['x0'], sv['h1'] = xc, h
        z, sv['ffn1'] = _ffn_forward(h, wl, 'ffn1', "_1", gather_job(l, 'then'), wl, parts['then'])
        w_in_l = _from_slabs(wl['w_in'])
        wl['w_in'] = jnp.concatenate([w_in_l[:, :lat], w_in_l[:, lat + MLA_ROPE:], w_in_l[:, lat:lat + MLA_ROPE],
                                      jnp.zeros((d_model, LANES - MLA_ROPE), BF16)], axis=1)
        wl['mla_w_uq'] = jnp.pad(_from_slabs(wl['mla_w_uq']).reshape(rq, h_mla, MLA_QK),
                                 ((0, 0), (0, 0), (0, MLA_QK_PAD - MLA_QK))).reshape(rq, h_mla * MLA_QK_PAD)
        wl['w_o'] = wl['w_o'].reshape(-1, d_model)
        sv['z1'] = z
        xc, h = _post_pre(xc, z, ffn1_post_g[l], 0.5, mix_pre_g[l], "post_pre_1")
        sv['x1'], sv['h2'] = xc, h

        proj = _mm([(h, wl['w_in'])], 'nn', BF16, "mix_in", tn=1408, tk=d_model)
        cqn = _rms_fwd(proj, mla_q_norm_g[l], "rms_cq", col_block=0)
        ckvn = _rms_fwd(proj, mla_kv_norm_g[l], "rms_ckv", col_block=1)
        qf = _rot([_mm([(cqn, wl['mla_w_uq'])], 'nn', F32, "mla_uq")], tab_q[0], s_a, "rope_q")
        kv = _mm([(ckvn, wl['mla_w_ukv'])], 'nn', BF16, "mla_ukv", b_slabs=True)
        kf = _mla_kfull(kv, proj, kr_block, tab_kr[0], h_mla, "mla_kfull")
        (oa, lse), got = _flash_fwd(qf, kf, kv, h_mla, "mla_attn", gather_job(l, 'ffn2'))
        wl.update(zip(parts['ffn2'], got))
        qd = _rot([proj], tab_d[0], s_p, "rope_qd", col_block=qd_block, width=wd_)
        kd = _rot([proj], tab_d[0], s_p, "rope_kd", col_block=qd_block + 1, width=wd_)
        vd = proj[:, lat + 2 * wd_:lat + 3 * wd_]
        outs = [_dil_fwd(qd, kd, vd, dil, "dil_attn_%d" % dil) for _, dil in DIL_PATTERNS]
        ob, mt = _dil_combine([o for o, _ in outs], [s for _, s in outs], "dil_combine")
        o = jnp.concatenate([oa, ob], axis=1)
        z = _mm([(o, wl['w_o'])], 'nn', F32, "mix_out", tk=o.shape[1])
        sv.update(proj=proj, cqn=cqn, ckvn=ckvn, qf=qf, kv=kv, kf=kf, oa=oa, lse=lse, qd=qd, kd=kd, vd=vd, ob=ob,
                  mt=mt, o=o, z2=z)
        xc, h = _post_pre(xc, z, mix_post_g[l], 1.0, ffn2_pre_g[l], "post_pre_2")
        sv['x2'], sv['h3'] = xc, h

        z, sv['ffn2'] = _ffn_forward(h, wl, 'ffn2', "_2", gather_job(l + 1, 'first'), layers[(l + 1) % depth],
                                     parts['first'])
        sv['z3'] = z
        if l + 1 < depth:
            xc, h = _post_pre(xc, z, ffn2_post_g[l], 0.5, ffn1_pre_g[l + 1], "post_pre_3")
        else:
            dy, loss = _post_loss(xc, z, ffn2_post_g[l], 0.5, loss_target[0], "post_loss")
        saved.append(sv)

    dx = dy
    gain_grads = {n: [None] * depth for n in GAIN_NAMES}
    shard_grads = [{} for _ in range(depth)]
    late_names = parts['mix'] + parts['ffn1']
    late = None
    for l in reversed(range(depth)):
        wl, sv, grads = layers[l], saved[l], {}

        dz, gain_grads['ffn2_post_g'][l] = _rms_bwd(sv['z3'], ffn2_post_g[l], dx, 0.5, "rms_bwd_post_f", out_dtype=BF16)
        dh = _ffn_backward(dz, sv['h3'], sv['ffn2'], wl, 'ffn2', grads, "_2")
        dx, gain_grads['ffn2_pre_g'][l] = _rms_bwd(sv['x2'], ffn2_pre_g[l], dh, 1.0, "rms_bwd_pre", res=dx)
        early = _reduce_begin([grads[n] for n in parts['ffn2']])

        dz, gain_grads['mix_post_g'][l] = _rms_bwd(sv['z2'], mix_post_g[l], dx, 1.0, "rms_bwd_post_m", out_dtype=BF16)
        do = _mm([(dz, wl['w_o'])], 'nt', BF16, "mix_out_dx", tk=d_model)
        grads['w_o'] = _mm([(sv['o'], dz)], 'tn', BF16, "mix_out_dw", tn=1024, tk=1024).reshape(N_CHIPS, -1, d_model)
        do_a, do_b = do[:, :h_mla * MLA_V], do[:, h_mla * MLA_V:]

        delta = _delta(do_a, sv['oa'], "delta_a")
        (dqf,), got = _flash_bwd_dq(sv['qf'], sv['kf'], sv['kv'], do_a, sv['lse'], delta, h_mla, "mla_attn_dq", early)
        shard_grads[l].update(zip(parts['ffn2'], _reduce_end(early, got)))
        (dkf, dva), got = _flash_bwd_dkv(sv['qf'], sv['kf'], sv['kv'], do_a, _stat_rows(sv['lse'], h_mla),
                                         _stat_rows(delta, h_mla), h_mla, "mla_attn_dkv", late[1] if late else None)
        if late:
            shard_grads[late[0]].update(zip(late_names, _reduce_end(late[1], got)))
        dq = _rot([dqf], tab_q[1], s_a, "rope_q_bwd")
        grads['mla_w_uq'] = _mm([(sv['cqn'], dq)], 'tn', BF16, "mla_uq_dw", tk=1024)
        dcqn = _mm([(dq, wl['mla_w_uq'])], 'nt', F32, "mla_uq_dx", tk=1024)
        dcq, gain_grads['mla_q_norm_g'][l] = _rms_bwd(sv['proj'], mla_q_norm_g[l], dcqn, 1.0, "rms_bwd_cq",
                                                      col_block=0, out_dtype=BF16)
        dkv, dkr = _mla_dkv(dkf, dva, tab_kr[1], h_mla, "mla_dkv")
        grads['mla_w_ukv'] = _mm([(sv['ckvn'], dkv)], 'tn', BF16, "mla_ukv_dw", tk=1024, out_slabs=N_CHIPS)
        dckvn = _mm([(dkv, wl['mla_w_ukv'])], 'nt', F32, "mla_ukv_dx", b_slabs=True)
        dckv, gain_grads['mla_kv_norm_g'][l] = _rms_bwd(sv['proj'], mla_kv_norm_g[l], dckvn, 1.0, "rms_bwd_ckv",
                                                        col_block=1, out_dtype=BF16)

        delta = _delta(do_b, sv['ob'], "delta_b")
        dqs, dks, dvs = [], [], []
        for _, dil in DIL_PATTERNS:
            dqs.append(_dil_bwd_dq(sv['qd'], sv['kd'], sv['vd'], do_b, sv['mt'], delta, dil, "dil_attn_dq_%d" % dil))
            dk_, dv_ = _dil_bwd_dkv(sv['qd'], sv['kd'], sv['vd'], do_b, sv['mt'], delta, dil, "dil_attn_dkv_%d" % dil)
            dks.append(dk_)
            dvs.append(dv_)
        dqd = _rot(dqs, tab_d[1], s_p, "rope_qd_bwd")
        dkd = _rot(dks, tab_d[1], s_p, "rope_kd_bwd")
        dvd = _rot(dvs, None, 0, "sum_dvd")
        dproj = jnp.concatenate([dcq, dckv, dqd, dkd, dvd, dkr], axis=1)
        grads['w_in'] = _mm([(sv['h2'], dproj)], 'tn', BF16, "mix_in_dw", tn=1408, tk=1024)
        dh = _mm([(dproj, wl['w_in'])], 'nt', F32, "mix_in_dx", tn=1024, tk=1408)
        dx, gain_grads['mix_pre_g'][l] = _rms_bwd(sv['x1'], mix_pre_g[l], dh, 1.0, "rms_bwd_pre", res=dx)

        dz, gain_grads['ffn1_post_g'][l] = _rms_bwd(sv['z1'], ffn1_post_g[l], dx, 0.5, "rms_bwd_post_f", out_dtype=BF16)
        dh = _ffn_backward(dz, sv['h1'], sv['ffn1'], wl, 'ffn1', grads, "_1")
        dx, gain_grads['ffn1_pre_g'][l] = _rms_bwd(sv['x0'], ffn1_pre_g[l], dh, 1.0, "rms_bwd_pre", res=dx)

        gp = grads['w_in']
        grads['w_in'] = _to_slabs(jnp.concatenate(
            [gp[:, :lat], gp[:, in_p - LANES:in_p - LANES + MLA_ROPE], gp[:, lat:lat + 3 * wd_]], axis=1))
        grads['mla_w_uq'] = _to_slabs(
            grads['mla_w_uq'].reshape(rq, h_mla, MLA_QK_PAD)[:, :, :MLA_QK].reshape(rq, h_mla * MLA_QK))
        late = (l, _reduce_begin([grads[n] for n in late_names]))
    shard_grads[0].update(zip(late_names, _reduce_end(late[1], _run_job(late[1], "rs_exchange_chips"))))

    gg = _pack_gains({n: jnp.stack(gain_grads[n]) for n in GAIN_NAMES})
    gg = _sum_blocks(_allgather_small(gg, "allgather_gain_grads"), N_DEV, "sum_gain_grads")

    grad_w, delta_w, new_m, new_v = {}, {}, {}, {}
    dg, mg, vg = _adamw(_pack_gains(ws), gg, _pack_gains(ms), _pack_gains(vs), "adamw_gains")
    for dst, slab in ((grad_w, gg), (delta_w, dg), (new_m, mg), (new_v, vg)):
        dst.update(_unpack_gains(slab, gain_shapes))
    for n in MATRIX_NAMES:
        shape = ws[n].shape
        g = jnp.stack([shard_grads[l][n] for l in range(depth)])
        flat = lambda a: a.reshape(shape[0] * shape[1], shape[2])
        d_, m_, v_ = _adamw(flat(ws[n]), flat(g), flat(ms[n]), flat(vs[n]), "adamw_" + n)
        grad_w[n], delta_w[n], new_m[n], new_v[n] = g, d_.reshape(shape), m_.reshape(shape), v_.reshape(shape)

    loss = lax.psum(loss[0, 0], ("x", "y", "c"))
    return (loss, dx[None], *[grad_w[n] for n in WEIGHT_NAMES], *[delta_w[n] for n in WEIGHT_NAMES],
            *[new_m[n] for n in WEIGHT_NAMES], *[new_v[n] for n in WEIGHT_NAMES])
```
